```python
import math
import jax, jax.numpy as jnp
from jax import lax
import numpy as np

D_MODEL = 2048
BATCH = 1
SEQ = 16384
DEPTH = 1

CHUNK = 64
Q_BLOCK = 128
FOX_HEADS = 8
FOX_HEAD_DIM = 128
GDN_HEADS = 8
GDN_HEAD_DIM = 128
GDN_CONV = 4
D_FF = 5632
FFN_CONV = 3
EPS = 1e-6

FOX_W = FOX_HEADS * FOX_HEAD_DIM
GDN_W = GDN_HEADS * GDN_HEAD_DIM
MIX_W = FOX_W + GDN_W
IN_SIZES = (3 * FOX_W, FOX_HEADS, 3 * GDN_W, GDN_HEADS, GDN_HEADS, GDN_W)
IN_COLS = 3 * FOX_W + FOX_HEADS + 4 * GDN_W + 2 * GDN_HEADS

kernel_name = "hybrid_fox_gdn_convffn_block"


def _split_at(x, sizes):
    idx, acc = [], 0
    for s in sizes[:-1]:
        acc += s
        idx.append(acc)
    return jnp.split(x, idx, axis=-1)


def rms_norm(x, gain):
    xf = x.astype(jnp.float32)
    y = xf * lax.rsqrt(jnp.mean(xf * xf, axis=-1, keepdims=True) + EPS)
    return (y * gain.astype(jnp.float32)).astype(x.dtype)


def l2_normalize(x):
    xf = x.astype(jnp.float32)
    return xf * lax.rsqrt(jnp.sum(xf * xf, axis=-1, keepdims=True) + EPS)


def modulate(h, shift, scale):
    return h * (1 + scale[:, None, :]) + shift[:, None, :]


def causal_dwconv(x, w, b=None):
    width, ch = w.shape
    y = lax.conv_general_dilated(
        x, w[:, None, :].astype(x.dtype), window_strides=(1,), padding=[(width - 1, 0)],
        dimension_numbers=("NWC", "WIO", "NWC"), feature_group_count=ch)
    return y if b is None else y + b.astype(x.dtype)


def forgetting_attention(q, k, v, f_logit, b_f):
    B, T, H, D = q.shape
    log_f = jax.nn.log_sigmoid(f_logit.astype(jnp.float32) + b_f.astype(jnp.float32))
    cum = jnp.cumsum(log_f, axis=1).transpose(0, 2, 1)
    scale = D ** -0.5
    outs = []
    for i in range(T // Q_BLOCK):
        q0, q1 = i * Q_BLOCK, (i + 1) * Q_BLOCK
        s = jnp.einsum("bqhd,bkhd->bhqk", q[:, q0:q1], k[:, :q1],
                       preferred_element_type=jnp.float32) * scale
        s = s + cum[:, :, q0:q1, None] - cum[:, :, None, :q1]
        causal = jnp.arange(q1)[None, :] <= jnp.arange(q0, q1)[:, None]
        p = jax.nn.softmax(jnp.where(causal, s, -jnp.inf), axis=-1)
        outs.append(jnp.einsum("bhqk,bkhd->bqhd", p.astype(v.dtype), v[:, :q1]))
    return jnp.concatenate(outs, axis=1)


def gated_delta_rule(q, k, v, g, beta):
    B, T, H, DK = q.shape
    DV = v.shape[-1]
    N = T // CHUNK
    q = q * DK ** -0.5

    def chunks(t):
        return t.reshape(B, N, CHUNK, H, *t.shape[3:]).swapaxes(2, 3)

    qc, kc, vc, bc = chunks(q), chunks(k), chunks(v), chunks(beta)
    gc = jnp.cumsum(chunks(g), axis=-1)
    tri_incl = jnp.tril(jnp.ones((CHUNK, CHUNK), bool))
    tri_strict = jnp.tril(jnp.ones((CHUNK, CHUNK), bool), -1)
    decay = jnp.exp(jnp.where(tri_incl, gc[..., :, None] - gc[..., None, :], -jnp.inf))

    k_beta = kc * bc[..., None]
    v_beta = vc * bc[..., None]
    low = jnp.where(tri_strict, jnp.einsum("bnhcd,bnhsd->bnhcs", k_beta, kc) * decay, 0.0)
    a_mat = low + jnp.eye(CHUNK, dtype=jnp.float32)
    rhs = jnp.concatenate([v_beta, k_beta * jnp.exp(gc)[..., None]], axis=-1)
    sol = lax.linalg.triangular_solve(a_mat, rhs, left_side=True, lower=True, unit_diagonal=True)
    u, w = sol[..., :DV], sol[..., DV:]

    attn = jnp.einsum("bnhcd,bnhsd->bnhcs", qc, kc) * decay
    q_dec = qc * jnp.exp(gc)[..., None]
    k_tail = kc * jnp.exp(gc[..., -1:] - gc)[..., None]
    g_last = jnp.exp(gc[..., -1])

    def step(state, inp):
        q_i, k_i, u_i, w_i, a_i, gl_i = inp
        v_new = u_i - jnp.einsum("bhck,bhkv->bhcv", w_i, state)
        o_i = (jnp.einsum("bhck,bhkv->bhcv", q_i, state)
               + jnp.einsum("bhcs,bhsv->bhcv", a_i, v_new))
        state = state * gl_i[..., None, None] + jnp.einsum("bhck,bhcv->bhkv", k_i, v_new)
        return state, o_i

    xs = tuple(jnp.moveaxis(t, 1, 0) for t in (q_dec, k_tail, u, w, attn, g_last))
    s0 = jnp.zeros((B, H, DK, DV), jnp.float32)
    _, o = lax.scan(step, s0, xs)
    return o.transpose(1, 0, 3, 2, 4).reshape(B, T, H, DV)


def hybrid_mixer(h, w_in, fox_forget_bias, fox_out_norm, gdn_conv_w, gdn_A_log, gdn_dt_bias,
                 gdn_out_norm, w_out):
    B, T, _ = h.shape
    proj = jnp.einsum("btd,de->bte", h, w_in)
    qkv_a, f_a, qkv_b, a_b, b_b, gate_b = _split_at(proj, IN_SIZES)

    qkv_a = qkv_a.reshape(B, T, 3, FOX_HEADS, FOX_HEAD_DIM)
    o_a = forgetting_attention(qkv_a[:, :, 0], qkv_a[:, :, 1], qkv_a[:, :, 2], f_a, fox_forget_bias)
    o_a = rms_norm(o_a, fox_out_norm).reshape(B, T, FOX_W)

    qkv_b = jax.nn.silu(causal_dwconv(qkv_b, gdn_conv_w)).reshape(B, T, 3, GDN_HEADS, GDN_HEAD_DIM)
    q_b = l2_normalize(qkv_b[:, :, 0])
    k_b = l2_normalize(qkv_b[:, :, 1])
    v_b = qkv_b[:, :, 2].astype(jnp.float32)
    beta = jax.nn.sigmoid(b_b.astype(jnp.float32))
    g = -jnp.exp(gdn_A_log.astype(jnp.float32)) * jax.nn.softplus(
        a_b.astype(jnp.float32) + gdn_dt_bias.astype(jnp.float32))
    o_b = gated_delta_rule(q_b, k_b, v_b, g, beta).astype(h.dtype)
    o_b = rms_norm(o_b, gdn_out_norm) * jax.nn.silu(gate_b.reshape(B, T, GDN_HEADS, GDN_HEAD_DIM))
    o_b = o_b.reshape(B, T, GDN_W)

    y = jnp.concatenate([o_a, o_b], axis=-1)
    return jnp.einsum("bte,ed->btd", y, w_out)


def conv_ffn(h, w_up, conv_w, conv_b, w_down):
    u = jnp.einsum("btd,df->btf", h, w_up)
    u = causal_dwconv(u, conv_w, conv_b)
    gate, val = jnp.split(u, 2, axis=-1)
    return jnp.einsum("btf,fd->btd", jax.nn.silu(gate) * val, w_down)


def setup_inputs(seed: int = 0) -> dict:
    key = jax.random.key(seed)
    ks = jax.random.split(key, 24)
    f32 = jnp.float32

    def nrm(k, shape, s):
        return s * jax.random.normal(k, shape, f32)

    def gain(k, n):
        return 1.0 + 0.05 * jax.random.normal(k, (DEPTH, n), f32)

    dt = jnp.exp(jax.random.uniform(ks[10], (DEPTH, GDN_HEADS), f32,
                                    minval=math.log(1e-3), maxval=math.log(1e-1)))
    return {
        "x": nrm(ks[0], (BATCH, SEQ, D_MODEL), 1.0),
        "c": nrm(ks[1], (BATCH, D_MODEL), 1.0),
        "w_ada": nrm(ks[2], (DEPTH, D_MODEL, 6 * D_MODEL), D_MODEL ** -0.5),
        "b_ada": nrm(ks[3], (DEPTH, 6 * D_MODEL), 0.02),
        "norm_mix_pre": gain(ks[4], D_MODEL),
        "norm_mix_post": gain(ks[5], D_MODEL),
        "w_in": nrm(ks[6], (DEPTH, D_MODEL, IN_COLS), D_MODEL ** -0.5),
        "fox_forget_bias": jax.random.uniform(ks[7], (DEPTH, FOX_HEADS), f32, minval=1.0, maxval=5.0),
        "fox_out_norm": gain(ks[8], FOX_HEAD_DIM),
        "gdn_conv_w": nrm(ks[9], (DEPTH, GDN_CONV, 3 * GDN_W), GDN_CONV ** -0.5),
        "gdn_A_log": jnp.log(jax.random.uniform(ks[11], (DEPTH, GDN_HEADS), f32, minval=1.0, maxval=16.0)),
        "gdn_dt_bias": dt + jnp.log(-jnp.expm1(-dt)),
        "gdn_out_norm": gain(ks[12], GDN_HEAD_DIM),
        "w_out": nrm(ks[13], (DEPTH, MIX_W, D_MODEL), MIX_W ** -0.5),
        "norm_ffn_pre": gain(ks[14], D_MODEL),
        "norm_ffn_post": gain(ks[15], D_MODEL),
        "w_up": nrm(ks[16], (DEPTH, D_MODEL, 2 * D_FF), D_MODEL ** -0.5),
        "ffn_conv_w": nrm(ks[17], (DEPTH, FFN_CONV, 2 * D_FF), FFN_CONV ** -0.5),
        "ffn_conv_b": nrm(ks[18], (DEPTH, 2 * D_FF), 0.02),
        "w_down": nrm(ks[19], (DEPTH, D_FF, D_MODEL), D_FF ** -0.5),
    }


def reference(x, c, w_ada, b_ada, norm_mix_pre, norm_mix_post, w_in, fox_forget_bias, fox_out_norm,
              gdn_conv_w, gdn_A_log, gdn_dt_bias, gdn_out_norm, w_out, norm_ffn_pre, norm_ffn_post,
              w_up, ffn_conv_w, ffn_conv_b, w_down):
    cond = jax.nn.silu(c)
    for l in range(DEPTH):
        mod = jnp.einsum("bd,de->be", cond, w_ada[l]) + b_ada[l]
        sh_m, sc_m, ga_m, sh_f, sc_f, ga_f = jnp.split(mod, 6, axis=-1)

        h = modulate(rms_norm(x, norm_mix_pre[l]), sh_m, sc_m)
        y = hybrid_mixer(h, w_in[l], fox_forget_bias[l], fox_out_norm[l], gdn_conv_w[l], gdn_A_log[l],
                         gdn_dt_bias[l], gdn_out_norm[l], w_out[l])
        x = x + ga_m[:, None, :] * rms_norm(y, norm_mix_post[l])

        h = modulate(rms_norm(x, norm_ffn_pre[l]), sh_f, sc_f)
        y = conv_ffn(h, w_up[l], ffn_conv_w[l], ffn_conv_b[l], w_down[l])
        x = x + ga_f[:, None, :] * rms_norm(y, norm_ffn_post[l])
    return x
```

```python
import functools
import math

import jax
import jax.numpy as jnp
from jax import lax
from jax.experimental import pallas as pl
from jax.experimental.pallas import tpu as pltpu

F32 = jnp.float32
BF16 = jnp.bfloat16

EPS = 1e-6
HEADS = 8
HEAD_DIM = 128
CHUNK = 64
GDN_CONV = 4
FFN_CONV = 3
LANES = 128
SUBLANES = 8
GROUP_W = HEADS * HEAD_DIM
LOG2E = math.log2(math.e)
NEG_INF = float("-inf")

F_LANE, A_LANE, B_LANE = 0, HEADS, 2 * HEADS


MIB = 1024 * 1024
V7X_VMEM_BYTES = 64 * MIB
TEMP_ALLOWANCE_BYTES = 22 * MIB


def _nbytes(shape, dtype):
    return math.prod(shape) * jnp.dtype(dtype).itemsize


def _cparams(semantics, window_bytes, scratch_bytes=0):
    want = 2 * window_bytes + scratch_bytes + TEMP_ALLOWANCE_BYTES
    return pltpu.CompilerParams(dimension_semantics=semantics,
                                vmem_limit_bytes=min(want, V7X_VMEM_BYTES - 4 * MIB))


def _sigmoid(x):
    return 1.0 / (1.0 + jnp.exp(-x))


def _silu(x):
    return x * _sigmoid(x)


def _split2(a):
    hi = a.astype(BF16)
    lo = (a - hi.astype(F32)).astype(BF16)
    return hi, lo


def _split3(a):
    hi = a.astype(BF16)
    r = a - hi.astype(F32)
    mid = r.astype(BF16)
    lo = (r - mid.astype(F32)).astype(BF16)
    return hi, mid, lo


def _dot(a, b):
    return jnp.dot(a, b, preferred_element_type=F32)


def _dot_nt(a, b):
    return lax.dot_general(a, b, (((1,), (1,)), ((), ())), preferred_element_type=F32)


def _dot_tn(a, b):
    return lax.dot_general(a, b, (((0,), (0,)), ((), ())), preferred_element_type=F32)


def _dot3(a, b):
    a_hi, a_lo = _split2(a)
    b_hi, b_lo = _split2(b)
    return _dot(a_hi, b_hi) + _dot(a_hi, b_lo) + _dot(a_lo, b_hi)


def _ada_kernel(c_ref, w_ref, b_ref, o_ref):
    cond = _silu(c_ref[...])
    o_ref[...] = jnp.sum(cond * w_ref[...], axis=0, keepdims=True) + b_ref[...]


def _ada_mod(c, w_ada, b_ada, tn=1024):
    d, n = w_ada.shape
    return pl.pallas_call(
        _ada_kernel,
        grid=(n // tn,),
        in_specs=[pl.BlockSpec((d, 1), lambda j: (0, 0)),
                  pl.BlockSpec((d, tn), lambda j: (0, j)),
                  pl.BlockSpec((1, tn), lambda j: (0, j))],
        out_specs=pl.BlockSpec((1, tn), lambda j: (0, j)),
        out_shape=jax.ShapeDtypeStruct((1, n), F32),
        compiler_params=_cparams(("arbitrary",), _nbytes((d, tn), F32)),
        name="ada",
    )(c.reshape(d, 1), w_ada, b_ada.reshape(1, n))


def _norm_modulate(x, gain, scale, shift):
    y = x * lax.rsqrt(jnp.mean(x * x, axis=-1, keepdims=True) + EPS) * gain
    return y * (1.0 + scale) + shift


def _inproj_kernel(x_ref, g_ref, sc_ref, sh_ref, wm_ref, ws_ref, om_ref, os_ref, h_ref, *, q_scale):
    j = pl.program_id(1)

    @pl.when(j == 0)
    def _():
        h = _norm_modulate(x_ref[...], g_ref[...], sc_ref[...], sh_ref[...])
        h_hi, h_lo = _split2(h)
        h_ref[...] = h_hi
        w_hi, w_lo = _split2(ws_ref[...])
        os_ref[...] = _dot(h_hi, w_hi) + _dot(h_hi, w_lo) + _dot(h_lo, w_hi)

    acc = _dot(h_ref[...], wm_ref[...])
    om_ref[...] = (acc * jnp.where(j == 0, q_scale, 1.0)).astype(BF16)


def _in_projection(x2, gain, scale, shift, w_main, w_small, tm=1024, tn=1024):
    t, d = x2.shape
    n = w_main.shape[1]
    row = lambda i, j: (i, 0)
    vec = lambda i, j: (0, 0)
    q_scale = HEAD_DIM ** -0.5 * LOG2E
    return pl.pallas_call(
        functools.partial(_inproj_kernel, q_scale=q_scale),
        grid=(t // tm, n // tn),
        in_specs=[pl.BlockSpec((tm, d), row),
                  pl.BlockSpec((1, d), vec), pl.BlockSpec((1, d), vec), pl.BlockSpec((1, d), vec),
                  pl.BlockSpec((d, tn), lambda i, j: (0, j)),
                  pl.BlockSpec((d, LANES), vec)],
        out_specs=[pl.BlockSpec((tm, tn), lambda i, j: (i, j)),
                   pl.BlockSpec((tm, LANES), row)],
        out_shape=[jax.ShapeDtypeStruct((t, n), BF16), jax.ShapeDtypeStruct((t, LANES), F32)],
        scratch_shapes=[pltpu.VMEM((tm, d), BF16)],
        compiler_params=_cparams(("arbitrary", "arbitrary"),
                                 _nbytes((tm, d), F32) + _nbytes((d, tn), BF16) + _nbytes((d, LANES), F32)
                                 + _nbytes((tm, tn), BF16) + _nbytes((tm, LANES), F32),
                                 _nbytes((tm, d), BF16)),
        name="inproj",
    )(x2, gain, scale, shift, w_main, w_small)


def _foxpre_kernel(s_ref, fb_ref, kx_ref, cs_ref, carry_ref, *, tb):
    i = pl.program_id(0)

    @pl.when(i == 0)
    def _():
        carry_ref[...] = jnp.zeros_like(carry_ref)

    z = s_ref[...] + fb_ref[...]
    log_f = jnp.minimum(z, 0.0) - jnp.log1p(jnp.exp(-jnp.abs(z)))
    r = lax.broadcasted_iota(jnp.int32, (tb, tb), 0)
    c = lax.broadcasted_iota(jnp.int32, (tb, tb), 1)
    tri = jnp.where(c <= r, 1.0, 0.0).astype(BF16)
    hi, mid, lo = _split3(log_f)
    cum3 = _dot(tri, jnp.concatenate([hi, mid, lo], axis=1))
    cum = cum3[:, :LANES] + cum3[:, LANES:2 * LANES] + cum3[:, 2 * LANES:]
    carry = carry_ref[...]
    cs_ref[...] = jnp.broadcast_to(carry * LOG2E, cs_ref.shape)
    carry_ref[...] = carry + cum[tb - 1:tb, :]

    dd = -cum * LOG2E
    lane = lax.broadcasted_iota(jnp.int32, (tb, HEAD_DIM), 1)
    for h in range(HEADS):
        col = jnp.broadcast_to(dd[:, F_LANE + h:F_LANE + h + 1], (tb, HEAD_DIM))
        c_hi = col.astype(BF16).astype(F32)
        c_mid = (col - c_hi).astype(BF16).astype(F32)
        c_lo = col - c_hi - c_mid
        kx = jnp.where(lane == 0, c_hi, jnp.where(lane == 1, c_mid, jnp.where(lane == 2, c_lo, 0.0)))
        kx_ref[:, h * HEAD_DIM:(h + 1) * HEAD_DIM] = kx.astype(BF16)


def _fox_prologue(small, fbias_row, tb):
    t = small.shape[0]
    nblk = t // tb
    return pl.pallas_call(
        functools.partial(_foxpre_kernel, tb=tb),
        grid=(nblk,),
        in_specs=[pl.BlockSpec((tb, LANES), lambda i: (i, 0)),
                  pl.BlockSpec((1, LANES), lambda i: (0, 0))],
        out_specs=[pl.BlockSpec((tb, GROUP_W), lambda i: (i, 0)),
                   pl.BlockSpec((SUBLANES, LANES), lambda i: (i, 0))],
        out_shape=[jax.ShapeDtypeStruct((t, GROUP_W), BF16),
                   jax.ShapeDtypeStruct((nblk * SUBLANES, LANES), F32)],
        scratch_shapes=[pltpu.VMEM((1, LANES), F32)],
        compiler_params=_cparams(("arbitrary",), _nbytes((tb, LANES), F32) + _nbytes((tb, GROUP_W), BF16)),
        name="foxpre",
    )(small, fbias_row)


def _fox_kernel(cs_ref, q_ref, k_ref, v_ref, kx_ref, gn_ref, o_ref, m_ref, acc_ref, *, tb, nblk):
    h = pl.program_id(0)
    i = pl.program_id(1)
    lane = lax.broadcasted_iota(jnp.int32, (tb, HEAD_DIM), 1)
    q_aug = jnp.concatenate([q_ref[...], jnp.where(lane < 3, 1.0, 0.0).astype(BF16)], axis=1)
    ones_col = jnp.where(lane == 0, 1.0, 0.0).astype(BF16)
    cq = cs_ref[h * nblk + i]

    m_ref[...] = jnp.full_like(m_ref, NEG_INF)
    acc_ref[...] = jnp.zeros_like(acc_ref)

    def step(j, masked):
        off = pl.multiple_of(j * tb, tb)
        k_aug = jnp.concatenate([k_ref[pl.ds(off, tb), :], kx_ref[pl.ds(off, tb), :]], axis=1)
        v_aug = jnp.concatenate([v_ref[pl.ds(off, tb), :], ones_col], axis=1)
        s = _dot_nt(q_aug, k_aug)
        if masked:
            r = lax.broadcasted_iota(jnp.int32, (tb, tb), 0)
            c = lax.broadcasted_iota(jnp.int32, (tb, tb), 1)
            s = jnp.where(c <= r, s, NEG_INF)
        delta = cq - cs_ref[h * nblk + j]
        m_old = m_ref[...]
        m_new = jnp.maximum(m_old, jnp.max(s, axis=1, keepdims=True) + delta)
        p = jnp.exp2(s - (m_new - delta))
        acc_ref[...] = acc_ref[...] * jnp.exp2(m_old - m_new) + _dot(p.astype(BF16), v_aug)
        m_ref[...] = m_new

    def body(j, carry):
        step(j, False)
        return carry

    lax.fori_loop(0, i, body, 0)
    step(i, True)

    acc = acc_ref[...]
    o = acc[:, :HEAD_DIM] / acc[:, HEAD_DIM:HEAD_DIM + 1]
    o = o * lax.rsqrt(jnp.mean(o * o, axis=-1, keepdims=True) + EPS) * gn_ref[...]
    o_ref[...] = o.astype(BF16)


def _fox_attention(cs_flat, main, kx, gnorm, tb):
    t = main.shape[0]
    nblk = t // tb
    grid_spec = pltpu.PrefetchScalarGridSpec(
        num_scalar_prefetch=1,
        grid=(HEADS, nblk),
        in_specs=[pl.BlockSpec((tb, HEAD_DIM), lambda h, i, cs: (i, h)),
                  pl.BlockSpec((t, HEAD_DIM), lambda h, i, cs: (0, HEADS + h)),
                  pl.BlockSpec((t, HEAD_DIM), lambda h, i, cs: (0, 2 * HEADS + h)),
                  pl.BlockSpec((t, HEAD_DIM), lambda h, i, cs: (0, h)),
                  pl.BlockSpec((1, HEAD_DIM), lambda h, i, cs: (0, 0))],
        out_specs=pl.BlockSpec((tb, HEAD_DIM), lambda h, i, cs: (i, h)),
        scratch_shapes=[pltpu.VMEM((tb, 1), F32), pltpu.VMEM((tb, 2 * HEAD_DIM), F32)],
    )
    return pl.pallas_call(
        functools.partial(_fox_kernel, tb=tb, nblk=nblk),
        grid_spec=grid_spec,
        out_shape=jax.ShapeDtypeStruct((t, GROUP_W), BF16),
        compiler_params=_cparams(("arbitrary", "arbitrary"),
                                 3 * _nbytes((t, HEAD_DIM), BF16) + 2 * _nbytes((tb, HEAD_DIM), BF16),
                                 _nbytes((tb, 3 * HEAD_DIM), F32)),
        name="fox",
    )(cs_flat, main, main, main, kx, gnorm)


def _unit_lower_inverse(low):
    r = lax.broadcasted_iota(jnp.int32, (CHUNK, CHUNK), 0)
    c = lax.broadcasted_iota(jnp.int32, (CHUNK, CHUNK), 1)
    inv = jnp.where(r == c, 1.0, 0.0) - low
    power = low
    for _ in range(int(math.log2(CHUNK)) - 1):
        power = _dot3(power, power)
        inv = inv + _dot3(inv, power)
    return inv


def _gdn_kernel(q_ref, k_ref, v_ref, gate_ref, s_ref, wq_ref, wk_ref, wv_ref, prm_ref, gn_ref, o_ref,
                xq_ref, xk_ref, xv_ref, yq_ref, yk_ref, yv_ref, state_ref, *, blk):
    i = pl.program_id(0)
    halo = SUBLANES

    @pl.when(i == 0)
    def _():
        state_ref[...] = jnp.zeros_like(state_ref)
        for x_ref in (xq_ref, xk_ref, xv_ref):
            x_ref[0:halo, :] = jnp.zeros((halo, GROUP_W), F32)

    def conv_silu(src_ref, x_ref, w_ref):
        x_ref[halo:halo + blk, :] = src_ref[...].astype(F32)
        y = w_ref[GDN_CONV - 1:GDN_CONV, :] * x_ref[halo:halo + blk, :]
        for tap in range(GDN_CONV - 1):
            shift = GDN_CONV - 1 - tap
            y = y + w_ref[tap:tap + 1, :] * x_ref[halo - shift:halo - shift + blk, :]
        x_ref[0:halo, :] = x_ref[blk:blk + halo, :]
        return _silu(y)

    def l2norm_heads(y, dst_ref, mult):
        for h in range(HEADS):
            yh = y[:, h * HEAD_DIM:(h + 1) * HEAD_DIM]
            inv = lax.rsqrt(jnp.sum(yh * yh, axis=-1, keepdims=True) + EPS) * mult
            dst_ref[:, h * HEAD_DIM:(h + 1) * HEAD_DIM] = yh * inv

    l2norm_heads(conv_silu(q_ref, xq_ref, wq_ref), yq_ref, HEAD_DIM ** -0.5)
    l2norm_heads(conv_silu(k_ref, xk_ref, wk_ref), yk_ref, 1.0)
    yv_ref[...] = conv_silu(v_ref, xv_ref, wv_ref)

    r = lax.broadcasted_iota(jnp.int32, (CHUNK, CHUNK), 0)
    c = lax.broadcasted_iota(jnp.int32, (CHUNK, CHUNK), 1)
    tri_incl = c <= r
    tri_strict = c < r
    tri_ones = jnp.where(tri_incl, 1.0, 0.0).astype(BF16)
    dt_bias = prm_ref[0:1, :]
    neg_a = -jnp.exp(prm_ref[1:2, :])
    gn = gn_ref[...]

    def chunk_body(ci, carry):
        r0 = pl.multiple_of(ci * CHUNK, CHUNK)
        rows = pl.ds(r0, CHUNK)
        slab = s_ref[rows, :]
        z = slab + dt_bias
        g = neg_a * (jnp.maximum(z, 0.0) + jnp.log1p(jnp.exp(-jnp.abs(z))))
        beta = _sigmoid(slab)
        g_hi, g_mid, g_lo = _split3(g)
        gc3 = _dot(tri_ones, jnp.concatenate([g_hi, g_mid, g_lo], axis=1))
        gc = gc3[:, :LANES] + gc3[:, LANES:2 * LANES] + gc3[:, 2 * LANES:]
        gc_t = gc.T

        for h in range(HEADS):
            cols = slice(h * HEAD_DIM, (h + 1) * HEAD_DIM)
            gcol = gc[:, A_LANE + h:A_LANE + h + 1]
            grow = gc_t[A_LANE + h:A_LANE + h + 1, :]
            decay = jnp.exp(jnp.where(tri_incl, gcol - grow, NEG_INF))
            g_last = grow[:, CHUNK - 1:CHUNK]
            e_gc = jnp.exp(gcol)
            e_tail = jnp.exp(g_last - gcol)
            bcol = beta[:, B_LANE + h:B_LANE + h + 1]

            qh = yq_ref[rows, cols]
            kh = yk_ref[rows, cols]
            vh = yv_ref[rows, cols]
            kb = kh * bcol
            vb = vh * bcol
            kh_b = kh.astype(BF16)
            low = jnp.where(tri_strict, _dot_nt(kb.astype(BF16), kh_b) * decay, 0.0)
            inv = _unit_lower_inverse(low)
            sol = _dot3(inv, jnp.concatenate([vb, kb * e_gc], axis=1))
            u = sol[:, :HEAD_DIM]
            w = sol[:, HEAD_DIM:]
            attn = _dot_nt(qh.astype(BF16), kh_b) * decay

            state = state_ref[h]
            wq = jnp.concatenate([w, qh * e_gc], axis=0).astype(BF16)
            ws_qs = _dot(wq, state.astype(BF16))
            v_new = u - ws_qs[:CHUNK]
            v_new_b = v_new.astype(BF16)
            o = ws_qs[CHUNK:] + _dot(attn.astype(BF16), v_new_b)
            state_ref[h] = state * jnp.exp(g_last) + _dot_tn((kh * e_tail).astype(BF16), v_new_b)

            gate = gate_ref[rows, cols].astype(F32)
            o = o * lax.rsqrt(jnp.mean(o * o, axis=-1, keepdims=True) + EPS) * gn * _silu(gate)
            o_ref[rows, cols] = o.astype(BF16)
        return carry

    lax.fori_loop(0, blk // CHUNK, chunk_body, 0)


def _gated_deltanet(main, small, conv_w, prm, gnorm, blk=512):
    t = main.shape[0]
    qcol = 3 * GROUP_W // GROUP_W
    big = lambda cb: pl.BlockSpec((blk, GROUP_W), lambda i: (i, cb))
    wspec = lambda cb: pl.BlockSpec((GDN_CONV, GROUP_W), lambda i: (0, cb))
    small_vec = lambda rows: pl.BlockSpec((rows, LANES), lambda i: (0, 0))
    return pl.pallas_call(
        functools.partial(_gdn_kernel, blk=blk),
        grid=(t // blk,),
        in_specs=[big(qcol), big(qcol + 1), big(qcol + 2), big(qcol + 3),
                  pl.BlockSpec((blk, LANES), lambda i: (i, 0)),
                  wspec(0), wspec(1), wspec(2),
                  small_vec(SUBLANES), small_vec(1)],
        out_specs=pl.BlockSpec((blk, GROUP_W), lambda i: (i, 0)),
        out_shape=jax.ShapeDtypeStruct((t, GROUP_W), BF16),
        scratch_shapes=[pltpu.VMEM((blk + SUBLANES, GROUP_W), F32)] * 3
                       + [pltpu.VMEM((blk, GROUP_W), F32)] * 3
                       + [pltpu.VMEM((HEADS, HEAD_DIM, HEAD_DIM), F32)],
        compiler_params=_cparams(("arbitrary",),
                                 5 * _nbytes((blk, GROUP_W), BF16) + _nbytes((blk, LANES), F32),
                                 6 * _nbytes((blk + SUBLANES, GROUP_W), F32)),
        name="gdn",
    )(main, main, main, main, small, conv_w, conv_w, conv_w, prm, gnorm)


def _outproj_kernel(oa_ref, ob_ref, w_ref, x_ref, g_ref, ga_ref, o_ref):
    y = _dot(oa_ref[...], w_ref[0:GROUP_W, :]) + _dot(ob_ref[...], w_ref[GROUP_W:2 * GROUP_W, :])
    yn = y * lax.rsqrt(jnp.mean(y * y, axis=-1, keepdims=True) + EPS) * g_ref[...]
    o_ref[...] = x_ref[...] + ga_ref[...] * yn


def _out_projection(o_a, o_b, w_out, x2, gain, gate, tm=512):
    t, d = x2.shape
    row = lambda i: (i, 0)
    vec = lambda i: (0, 0)
    return pl.pallas_call(
        _outproj_kernel,
        grid=(t // tm,),
        in_specs=[pl.BlockSpec((tm, GROUP_W), row), pl.BlockSpec((tm, GROUP_W), row),
                  pl.BlockSpec((2 * GROUP_W, d), vec),
                  pl.BlockSpec((tm, d), row),
                  pl.BlockSpec((1, d), vec), pl.BlockSpec((1, d), vec)],
        out_specs=pl.BlockSpec((tm, d), row),
        out_shape=jax.ShapeDtypeStruct((t, d), F32),
        compiler_params=_cparams(("arbitrary",),
                                 2 * _nbytes((tm, GROUP_W), BF16) + _nbytes((2 * GROUP_W, d), BF16)
                                 + 2 * _nbytes((tm, d), F32)),
        name="outproj",
    )(o_a, o_b, w_out, x2, gain, gate)


def _ffn_kernel(x_ref, g_ref, sc_ref, sh_ref, wg_ref, wv_ref, cwg_ref, cwv_ref, cbg_ref, cbv_ref, wd_ref,
                pg_ref, ga_ref, o_ref, h_ref, acc_ref, ug_ref, uv_ref, cg_ref, cv_ref, *, tm, nf):
    i = pl.program_id(0)
    j = pl.program_id(1)
    halo = SUBLANES

    @pl.when(j == 0)
    def _():
        h_ref[...] = _norm_modulate(x_ref[...], g_ref[...], sc_ref[...], sh_ref[...]).astype(BF16)
        acc_ref[...] = jnp.zeros_like(acc_ref)

    @pl.when(i == 0)
    def _():
        cg_ref[j] = jnp.zeros(cg_ref.shape[1:], F32)
        cv_ref[j] = jnp.zeros(cv_ref.shape[1:], F32)

    def up_conv(w_ref, u_ref, carry_ref, cw_ref, cb_ref):
        u = _dot(h_ref[...], w_ref[...])
        u_ref[0:halo, :] = carry_ref[j]
        u_ref[halo:halo + tm, :] = u
        carry_ref[j] = u[tm - halo:tm, :]
        y = cw_ref[FFN_CONV - 1:FFN_CONV, :] * u + cb_ref[...]
        for tap in range(FFN_CONV - 1):
            shift = FFN_CONV - 1 - tap
            y = y + cw_ref[tap:tap + 1, :] * u_ref[halo - shift:halo - shift + tm, :]
        return y

    gate = up_conv(wg_ref, ug_ref, cg_ref, cwg_ref, cbg_ref)
    val = up_conv(wv_ref, uv_ref, cv_ref, cwv_ref, cbv_ref)
    act = (_silu(gate) * val).astype(BF16)
    acc_ref[...] += _dot(act, wd_ref[...])

    @pl.when(j == nf - 1)
    def _():
        y = acc_ref[...]
        yn = y * lax.rsqrt(jnp.mean(y * y, axis=-1, keepdims=True) + EPS) * pg_ref[...]
        o_ref[...] = x_ref[...] + ga_ref[...] * yn


def _conv_ffn(x2, gain, scale, shift, w_up, conv_w, conv_b, w_down, post_gain, gate, tm=512, tf=512):
    t, d = x2.shape
    d_ff = w_down.shape[0]
    nf = d_ff // tf
    row = lambda i, j: (i, 0)
    vec = lambda i, j: (0, 0)
    gcol = lambda i, j: (0, j)
    vcol = lambda i, j: (0, nf + j)
    return pl.pallas_call(
        functools.partial(_ffn_kernel, tm=tm, nf=nf),
        grid=(t // tm, nf),
        in_specs=[pl.BlockSpec((tm, d), row),
                  pl.BlockSpec((1, d), vec), pl.BlockSpec((1, d), vec), pl.BlockSpec((1, d), vec),
                  pl.BlockSpec((d, tf), gcol), pl.BlockSpec((d, tf), vcol),
                  pl.BlockSpec((FFN_CONV, tf), gcol), pl.BlockSpec((FFN_CONV, tf), vcol),
                  pl.BlockSpec((1, tf), gcol), pl.BlockSpec((1, tf), vcol),
                  pl.BlockSpec((tf, d), lambda i, j: (j, 0)),
                  pl.BlockSpec((1, d), vec), pl.BlockSpec((1, d), vec)],
        out_specs=pl.BlockSpec((tm, d), row),
        out_shape=jax.ShapeDtypeStruct((t, d), F32),
        scratch_shapes=[pltpu.VMEM((tm, d), BF16), pltpu.VMEM((tm, d), F32),
                        pltpu.VMEM((tm + SUBLANES, tf), F32), pltpu.VMEM((tm + SUBLANES, tf), F32),
                        pltpu.VMEM((nf, SUBLANES, tf), F32), pltpu.VMEM((nf, SUBLANES, tf), F32)],
        compiler_params=_cparams(("arbitrary", "arbitrary"),
                                 2 * _nbytes((tm, d), F32) + 3 * _nbytes((d, tf), BF16),
                                 _nbytes((tm, d), BF16) + _nbytes((tm, d), F32)
                                 + 2 * _nbytes((tm + SUBLANES, tf), F32) + 2 * _nbytes((nf, SUBLANES, tf), F32)),
        name="ffn",
    )(x2, gain, scale, shift, w_up, w_up, conv_w, conv_w, conv_b, conv_b, w_down, post_gain, gate)


def _lane_row(vals, lane0):
    return jnp.zeros((1, LANES), F32).at[0, lane0:lane0 + HEADS].set(vals.astype(F32))


def _layer(x2, cond, p, attn_block):
    d = x2.shape[1]
    mod = _ada_mod(cond, p["w_ada"], p["b_ada"])
    sh_m, sc_m, ga_m, sh_f, sc_f, ga_f = (mod[:, k * d:(k + 1) * d] for k in range(6))

    w_in = p["w_in"]
    o_f = 3 * GROUP_W
    o_g = o_f + HEADS
    o_a = o_g + 3 * GROUP_W
    o_gate = o_a + 2 * HEADS
    w_main = jnp.concatenate([w_in[:, :o_f], w_in[:, o_g:o_a], w_in[:, o_gate:]], axis=1).astype(BF16)
    w_small = jnp.concatenate([w_in[:, o_f:o_g], w_in[:, o_a:o_gate],
                               jnp.zeros((d, LANES - 3 * HEADS), F32)], axis=1)

    main, small = _in_projection(x2, p["norm_mix_pre"].reshape(1, d), sc_m, sh_m, w_main, w_small)

    kx, cs = _fox_prologue(small, _lane_row(p["fox_forget_bias"], F_LANE), attn_block)
    nblk = x2.shape[0] // attn_block
    cs_flat = cs.reshape(nblk, SUBLANES, LANES)[:, 0, F_LANE:F_LANE + HEADS].T.reshape(-1)
    o_fox = _fox_attention(cs_flat, main, kx, p["fox_out_norm"].reshape(1, HEAD_DIM), attn_block)

    prm = jnp.concatenate([_lane_row(p["gdn_dt_bias"], A_LANE), _lane_row(p["gdn_A_log"], A_LANE),
                           jnp.zeros((SUBLANES - 2, LANES), F32)], axis=0)
    o_gdn = _gated_deltanet(main, small, p["gdn_conv_w"], prm, p["gdn_out_norm"].reshape(1, HEAD_DIM))

    x2 = _out_projection(o_fox, o_gdn, p["w_out"].astype(BF16), x2, p["norm_mix_post"].reshape(1, d), ga_m)

    return _conv_ffn(x2, p["norm_ffn_pre"].reshape(1, d), sc_f, sh_f, p["w_up"].astype(BF16),
                     p["ffn_conv_w"], p["ffn_conv_b"].reshape(1, -1), p["w_down"].astype(BF16),
                     p["norm_ffn_post"].reshape(1, d), ga_f)


def kernel(x, c, w_ada, b_ada, norm_mix_pre, norm_mix_post, w_in, fox_forget_bias, fox_out_norm, gdn_conv_w, gdn_A_log, gdn_dt_bias, gdn_out_norm, w_out, norm_ffn_pre, norm_ffn_post, w_up, ffn_conv_w, ffn_conv_b, w_down):
    b, t, d = x.shape
    assert b == 1, "single-sequence kernel"
    params = dict(w_ada=w_ada, b_ada=b_ada, norm_mix_pre=norm_mix_pre, norm_mix_post=norm_mix_post, w_in=w_in,
                  fox_forget_bias=fox_forget_bias, fox_out_norm=fox_out_norm, gdn_conv_w=gdn_conv_w,
                  gdn_A_log=gdn_A_log, gdn_dt_bias=gdn_dt_bias, gdn_out_norm=gdn_out_norm, w_out=w_out,
                  norm_ffn_pre=norm_ffn_pre, norm_ffn_post=norm_ffn_post, w_up=w_up, ffn_conv_w=ffn_conv_w,
                  ffn_conv_b=ffn_conv_b, w_down=w_down)
    attn_block = min(512, t)
    x2 = x.reshape(t, d)
    for l in range(w_ada.shape[0]):
        x2 = _layer(x2, c, {k: v[l] for k, v in params.items()}, attn_block)
    return x2.reshape(b, t, d)
```

```python
import functools
import math

import jax
import jax.numpy as jnp
from jax import lax
from jax.experimental import pallas as pl
from jax.experimental.pallas import tpu as pltpu

F32 = jnp.float32
BF16 = jnp.bfloat16

EPS = 1e-6
HEADS = 8
HEAD_DIM = 128
CHUNK = 64
GDN_CONV = 4
FFN_CONV = 3
LANES = 128
SUBLANES = 8
GROUP_W = HEADS * HEAD_DIM
LOG2E = math.log2(math.e)
NEG_INF = float("-inf")

F_LANE, A_LANE, B_LANE = 0, HEADS, 2 * HEADS


MIB = 1024 * 1024
V7X_VMEM_BYTES = 64 * MIB
TEMP_ALLOWANCE_BYTES = 22 * MIB


def _nbytes(shape, dtype):
    return math.prod(shape) * jnp.dtype(dtype).itemsize


def _cparams(semantics, window_bytes, scratch_bytes=0):
    want = 2 * window_bytes + scratch_bytes + TEMP_ALLOWANCE_BYTES
    return pltpu.CompilerParams(dimension_semantics=semantics,
                                vmem_limit_bytes=min(want, V7X_VMEM_BYTES - 4 * MIB))


def _sigmoid(x):
    return 1.0 / (1.0 + jnp.exp(-x))


def _silu(x):
    return x * _sigmoid(x)


def _split2(a):
    hi = a.astype(BF16)
    lo = (a - hi.astype(F32)).astype(BF16)
    return hi, lo


def _split3(a):
    hi = a.astype(BF16)
    r = a - hi.astype(F32)
    mid = r.astype(BF16)
    lo = (r - mid.astype(F32)).astype(BF16)
    return hi, mid, lo


def _dot(a, b):
    return jnp.dot(a, b, preferred_element_type=F32)


def _dot_nt(a, b):
    return lax.dot_general(a, b, (((1,), (1,)), ((), ())), preferred_element_type=F32)


def _dot_tn(a, b):
    return lax.dot_general(a, b, (((0,), (0,)), ((), ())), preferred_element_type=F32)


def _ada_kernel(c_ref, w_ref, b_ref, o_ref):
    cond = _silu(c_ref[...])
    o_ref[...] = jnp.sum(cond * w_ref[...], axis=0, keepdims=True) + b_ref[...]


def _ada_mod(c, w_ada, b_ada, tn=1024):
    d, n = w_ada.shape
    return pl.pallas_call(
        _ada_kernel,
        grid=(n // tn,),
        in_specs=[pl.BlockSpec((d, 1), lambda j: (0, 0)),
                  pl.BlockSpec((d, tn), lambda j: (0, j)),
                  pl.BlockSpec((1, tn), lambda j: (0, j))],
        out_specs=pl.BlockSpec((1, tn), lambda j: (0, j)),
        out_shape=jax.ShapeDtypeStruct((1, n), F32),
        compiler_params=_cparams(("arbitrary",), _nbytes((d, tn), F32)),
        name="ada",
    )(c.reshape(d, 1), w_ada, b_ada.reshape(1, n))


def _norm_modulate(x, gain, scale, shift):
    y = x * lax.rsqrt(jnp.mean(x * x, axis=-1, keepdims=True) + EPS) * gain
    return y * (1.0 + scale) + shift


def _inproj_kernel(x_ref, g_ref, sc_ref, sh_ref, wm_ref, ws_ref, om_ref, os_ref, h_ref, *, q_scale):
    j = pl.program_id(1)

    @pl.when(j == 0)
    def _():
        h = _norm_modulate(x_ref[...], g_ref[...], sc_ref[...], sh_ref[...])
        h_hi, h_lo = _split2(h)
        h_ref[...] = h_hi
        w_hi, w_lo = _split2(ws_ref[...])
        os_ref[...] = _dot(h_hi, w_hi) + _dot(h_hi, w_lo) + _dot(h_lo, w_hi)

    acc = _dot(h_ref[...], wm_ref[...])
    om_ref[...] = (acc * jnp.where(j == 0, q_scale, 1.0)).astype(BF16)


def _in_projection(x2, gain, scale, shift, w_main, w_small, tm=1024, tn=1024):
    t, d = x2.shape
    n = w_main.shape[1]
    row = lambda i, j: (i, 0)
    vec = lambda i, j: (0, 0)
    q_scale = HEAD_DIM ** -0.5 * LOG2E
    return pl.pallas_call(
        functools.partial(_inproj_kernel, q_scale=q_scale),
        grid=(t // tm, n // tn),
        in_specs=[pl.BlockSpec((tm, d), row),
                  pl.BlockSpec((1, d), vec), pl.BlockSpec((1, d), vec), pl.BlockSpec((1, d), vec),
                  pl.BlockSpec((d, tn), lambda i, j: (0, j)),
                  pl.BlockSpec((d, LANES), vec)],
        out_specs=[pl.BlockSpec((tm, tn), lambda i, j: (i, j)),
                   pl.BlockSpec((tm, LANES), row)],
        out_shape=[jax.ShapeDtypeStruct((t, n), BF16), jax.ShapeDtypeStruct((t, LANES), F32)],
        scratch_shapes=[pltpu.VMEM((tm, d), BF16)],
        compiler_params=_cparams(("arbitrary", "arbitrary"),
                                 _nbytes((tm, d), F32) + _nbytes((d, tn), BF16) + _nbytes((d, LANES), F32)
                                 + _nbytes((tm, tn), BF16) + _nbytes((tm, LANES), F32),
                                 _nbytes((tm, d), BF16)),
        name="inproj",
    )(x2, gain, scale, shift, w_main, w_small)


def _foxpre_kernel(s_ref, fb_ref, kx_ref, cs_ref, carry_ref, *, tb):
    i = pl.program_id(0)

    @pl.when(i == 0)
    def _():
        carry_ref[...] = jnp.zeros_like(carry_ref)

    z = s_ref[...] + fb_ref[...]
    log_f = jnp.minimum(z, 0.0) - jnp.log1p(jnp.exp(-jnp.abs(z)))
    r = lax.broadcasted_iota(jnp.int32, (tb, tb), 0)
    c = lax.broadcasted_iota(jnp.int32, (tb, tb), 1)
    tri = jnp.where(c <= r, 1.0, 0.0).astype(BF16)
    hi, mid, lo = _split3(log_f)
    cum3 = _dot(tri, jnp.concatenate([hi, mid, lo], axis=1))
    cum = cum3[:, :LANES] + cum3[:, LANES:2 * LANES] + cum3[:, 2 * LANES:]
    carry = carry_ref[...]
    cs_ref[...] = jnp.broadcast_to(carry * LOG2E, cs_ref.shape)
    carry_ref[...] = carry + cum[tb - 1:tb, :]

    dd = -cum * LOG2E
    lane = lax.broadcasted_iota(jnp.int32, (tb, HEAD_DIM), 1)
    for h in range(HEADS):
        col = jnp.broadcast_to(dd[:, F_LANE + h:F_LANE + h + 1], (tb, HEAD_DIM))
        c_hi = col.astype(BF16).astype(F32)
        c_mid = (col - c_hi).astype(BF16).astype(F32)
        c_lo = col - c_hi - c_mid
        kx = jnp.where(lane == 0, c_hi, jnp.where(lane == 1, c_mid, jnp.where(lane == 2, c_lo, 0.0)))
        kx_ref[:, h * HEAD_DIM:(h + 1) * HEAD_DIM] = kx.astype(BF16)


def _fox_prologue(small, fbias_row, tb):
    t = small.shape[0]
    nblk = t // tb
    return pl.pallas_call(
        functools.partial(_foxpre_kernel, tb=tb),
        grid=(nblk,),
        in_specs=[pl.BlockSpec((tb, LANES), lambda i: (i, 0)),
                  pl.BlockSpec((1, LANES), lambda i: (0, 0))],
        out_specs=[pl.BlockSpec((tb, GROUP_W), lambda i: (i, 0)),
                   pl.BlockSpec((SUBLANES, LANES), lambda i: (i, 0))],
        out_shape=[jax.ShapeDtypeStruct((t, GROUP_W), BF16),
                   jax.ShapeDtypeStruct((nblk * SUBLANES, LANES), F32)],
        scratch_shapes=[pltpu.VMEM((1, LANES), F32)],
        compiler_params=_cparams(("arbitrary",), _nbytes((tb, LANES), F32) + _nbytes((tb, GROUP_W), BF16)),
        name="foxpre",
    )(small, fbias_row)


V_ROWS = HEAD_DIM + 16


def _fox_kernel(cs_ref, q_ref, k_ref, v_ref, kx_ref, gn_ref, o_ref,
                vt_ref, qt_ref, sa_ref, sb_ref, m_ref, acc_ref, *, tb, nblk):
    h = pl.program_id(0)
    i = pl.program_id(1)

    @pl.when(i == 0)
    def _():
        row = lax.broadcasted_iota(jnp.int32, (V_ROWS - HEAD_DIM, tb), 0)
        tail = jnp.where(row == 0, 1.0, 0.0).astype(BF16)

        def fill(j, carry):
            off = pl.multiple_of(j * tb, tb)
            vt = v_ref[pl.ds(off, tb), :].astype(F32).T.astype(BF16)
            vt_ref[j] = jnp.concatenate([vt, tail], axis=0)
            return carry

        lax.fori_loop(0, nblk, fill, 0)

    row = lax.broadcasted_iota(jnp.int32, (HEAD_DIM, tb), 0)
    qt_ref[0:HEAD_DIM, :] = q_ref[...].astype(F32).T.astype(BF16)
    qt_ref[HEAD_DIM:2 * HEAD_DIM, :] = jnp.where(row < 3, 1.0, 0.0).astype(BF16)
    m_ref[...] = jnp.full_like(m_ref, NEG_INF)
    acc_ref[...] = jnp.zeros_like(acc_ref)
    cq = cs_ref[h * nblk + i]

    def scores(j, s_ref):
        off = pl.multiple_of(j * tb, tb)
        k_aug = jnp.concatenate([k_ref[pl.ds(off, tb), :], kx_ref[pl.ds(off, tb), :]], axis=1)
        s_ref[...] = _dot(k_aug, qt_ref[...])

    def softmax_pv(j, s_ref, masked):
        s = s_ref[...]
        if masked:
            key = lax.broadcasted_iota(jnp.int32, (tb, tb), 0)
            qry = lax.broadcasted_iota(jnp.int32, (tb, tb), 1)
            s = jnp.where(key <= qry, s, NEG_INF)
        delta = cq - cs_ref[h * nblk + j]
        m_old = m_ref[...]
        m_new = jnp.maximum(m_old, jnp.max(s, axis=0, keepdims=True) + delta)
        p = jnp.exp2(s - (m_new - delta))
        acc_ref[...] = acc_ref[...] * jnp.exp2(m_old - m_new) + _dot(vt_ref[j], p.astype(BF16))
        m_ref[...] = m_new

    scores(0, sa_ref)

    def pair_body(jj, carry):
        j0 = 2 * jj
        scores(j0 + 1, sb_ref)
        softmax_pv(j0, sa_ref, False)
        scores(j0 + 2, sa_ref)
        softmax_pv(j0 + 1, sb_ref, False)
        return carry

    lax.fori_loop(0, lax.shift_right_logical(i, 1), pair_body, 0)
    odd = lax.bitwise_and(i, 1)

    @pl.when(odd == 0)
    def _():
        softmax_pv(i, sa_ref, True)

    @pl.when(odd == 1)
    def _():
        scores(i, sb_ref)
        softmax_pv(i - 1, sa_ref, False)
        softmax_pv(i, sb_ref, True)

    acc = acc_ref[...]
    o = acc[:HEAD_DIM, :] / acc[HEAD_DIM:HEAD_DIM + 1, :]
    o = o * lax.rsqrt(jnp.mean(o * o, axis=0, keepdims=True) + EPS) * gn_ref[...]
    o_ref[...] = o.T.astype(BF16)


def _fox_attention(cs_flat, main, kx, gnorm_col, tb):
    t = main.shape[0]
    nblk = t // tb
    grid_spec = pltpu.PrefetchScalarGridSpec(
        num_scalar_prefetch=1,
        grid=(HEADS, nblk),
        in_specs=[pl.BlockSpec((tb, HEAD_DIM), lambda h, i, cs: (i, h)),
                  pl.BlockSpec((t, HEAD_DIM), lambda h, i, cs: (0, HEADS + h)),
                  pl.BlockSpec((t, HEAD_DIM), lambda h, i, cs: (0, 2 * HEADS + h)),
                  pl.BlockSpec((t, HEAD_DIM), lambda h, i, cs: (0, h)),
                  pl.BlockSpec((HEAD_DIM, 1), lambda h, i, cs: (0, 0))],
        out_specs=pl.BlockSpec((tb, HEAD_DIM), lambda h, i, cs: (i, h)),
        scratch_shapes=[pltpu.VMEM((nblk, V_ROWS, tb), BF16), pltpu.VMEM((2 * HEAD_DIM, tb), BF16),
                        pltpu.VMEM((tb, tb), F32), pltpu.VMEM((tb, tb), F32),
                        pltpu.VMEM((1, tb), F32), pltpu.VMEM((V_ROWS, tb), F32)],
    )
    return pl.pallas_call(
        functools.partial(_fox_kernel, tb=tb, nblk=nblk),
        grid_spec=grid_spec,
        out_shape=jax.ShapeDtypeStruct((t, GROUP_W), BF16),
        compiler_params=_cparams(("arbitrary", "arbitrary"),
                                 3 * _nbytes((t, HEAD_DIM), BF16) + 2 * _nbytes((tb, HEAD_DIM), BF16),
                                 _nbytes((nblk, V_ROWS, tb), BF16) + _nbytes((2 * HEAD_DIM, tb), BF16)
                                 + 2 * _nbytes((tb, tb), F32) + _nbytes((V_ROWS + SUBLANES, tb), F32)),
        name="fox",
    )(cs_flat, main, main, main, kx, gnorm_col)


def _gdn_kernel(q_ref, k_ref, v_ref, gate_ref, s_ref, wq_ref, wk_ref, wv_ref, prm_ref, gn_ref, o_ref,
                xq_ref, xk_ref, xv_ref, yq_ref, yk_ref, yv_ref, state_ref, *, blk):
    i = pl.program_id(0)
    halo = SUBLANES

    @pl.when(i == 0)
    def _():
        state_ref[...] = jnp.zeros_like(state_ref)
        for x_ref in (xq_ref, xk_ref, xv_ref):
            x_ref[0:halo, :] = jnp.zeros((halo, GROUP_W), F32)

    def conv_silu(src_ref, x_ref, w_ref):
        x_ref[halo:halo + blk, :] = src_ref[...].astype(F32)
        y = w_ref[GDN_CONV - 1:GDN_CONV, :] * x_ref[halo:halo + blk, :]
        for tap in range(GDN_CONV - 1):
            shift = GDN_CONV - 1 - tap
            y = y + w_ref[tap:tap + 1, :] * x_ref[halo - shift:halo - shift + blk, :]
        x_ref[0:halo, :] = x_ref[blk:blk + halo, :]
        return _silu(y)

    def l2norm_heads(y, dst_ref, mult):
        for h in range(HEADS):
            yh = y[:, h * HEAD_DIM:(h + 1) * HEAD_DIM]
            inv = lax.rsqrt(jnp.sum(yh * yh, axis=-1, keepdims=True) + EPS) * mult
            dst_ref[:, h * HEAD_DIM:(h + 1) * HEAD_DIM] = yh * inv

    l2norm_heads(conv_silu(q_ref, xq_ref, wq_ref), yq_ref, HEAD_DIM ** -0.5)
    l2norm_heads(conv_silu(k_ref, xk_ref, wk_ref), yk_ref, 1.0)
    yv_ref[...] = conv_silu(v_ref, xv_ref, wv_ref)

    row = lax.broadcasted_iota(jnp.int32, (CHUNK, LANES), 0)
    lane = lax.broadcasted_iota(jnp.int32, (CHUNK, LANES), 1)
    left = lane < CHUNK
    lane_in = jnp.where(left, lane, lane - CHUNK)
    tri_incl = lane_in <= row
    tri_strict = lane_in < row
    eye_pair = jnp.where(lane_in == row, 1.0, 0.0)
    m_left = jnp.where(left, 1.0, 0.0).astype(BF16)
    m_right = jnp.where(left, 0.0, 1.0).astype(BF16)
    left_row = left[0:1, :]
    r2 = lax.broadcasted_iota(jnp.int32, (2 * CHUNK, CHUNK), 0)
    c2 = lax.broadcasted_iota(jnp.int32, (2 * CHUNK, CHUNK), 1)
    tri_twice = jnp.where(c2 <= jnp.where(r2 < CHUNK, r2, r2 - CHUNK), 1.0, 0.0).astype(BF16)
    dt_bias = prm_ref[0:1, :]
    neg_a = -jnp.exp(prm_ref[1:2, :])
    gn = gn_ref[...]
    pairs = [(2 * p, 2 * p + 1) for p in range(HEADS // 2)]

    def blockdiag(x_b):
        return jnp.concatenate([x_b * m_left, x_b * m_right], axis=0)

    def split_bd(x):
        hi, lo = _split2(x)
        return hi, lo, blockdiag(hi), blockdiag(lo)

    def pair_dot3(a_hi, a_lo, b_hi_bd, b_lo_bd):
        return _dot(jnp.concatenate([a_hi, a_hi, a_lo], axis=1),
                    jnp.concatenate([b_hi_bd, b_lo_bd, b_hi_bd], axis=0))

    def blockdiag2(a, b):
        z = jnp.zeros_like(a)
        return jnp.concatenate([jnp.concatenate([a, z], axis=1), jnp.concatenate([z, b], axis=1)], axis=0)

    def chunk_body(ci, carry):
        r0 = pl.multiple_of(ci * CHUNK, CHUNK)
        rows = pl.ds(r0, CHUNK)
        slab = s_ref[rows, :]
        z = slab + dt_bias
        g = neg_a * (jnp.maximum(z, 0.0) + jnp.log1p(jnp.exp(-jnp.abs(z))))
        beta = _sigmoid(slab)
        g_hi, g_mid, g_lo = _split3(g)
        gc3 = _dot(tri_twice, jnp.concatenate([g_hi, g_mid, g_lo], axis=1))
        gc2 = gc3[:, :LANES] + gc3[:, LANES:2 * LANES] + gc3[:, 2 * LANES:]
        gc = gc2[:CHUNK]
        gc2_t = gc2.T

        hs = range(HEADS)
        cols = [slice(h * HEAD_DIM, (h + 1) * HEAD_DIM) for h in hs]
        gcol = [gc[:, A_LANE + h:A_LANE + h + 1] for h in hs]
        g_last = [gc[CHUNK - 1:CHUNK, A_LANE + h:A_LANE + h + 1] for h in hs]
        e_gc = [jnp.exp(gcol[h]) for h in hs]
        e_tail = [jnp.exp(g_last[h] - gcol[h]) for h in hs]
        bcol = [beta[:, B_LANE + h:B_LANE + h + 1] for h in hs]
        q = [yq_ref[rows, cols[h]] for h in hs]
        k = [yk_ref[rows, cols[h]] for h in hs]
        kb = [k[h] * bcol[h] for h in hs]
        vb = [yv_ref[rows, cols[h]] * bcol[h] for h in hs]
        k_b = [k[h].astype(BF16) for h in hs]

        decay, low, attn = [], [], []
        for a, b in pairs:
            diff = (jnp.where(left, gcol[a], gcol[b])
                    - jnp.where(left_row, gc2_t[A_LANE + a:A_LANE + a + 1, :], gc2_t[A_LANE + b:A_LANE + b + 1, :]))
            decay.append(jnp.exp(jnp.where(tri_incl, diff, NEG_INF)))
        for p, (a, b) in enumerate(pairs):
            lhs = jnp.concatenate([jnp.concatenate([kb[a], kb[b]], axis=1),
                                   jnp.concatenate([q[a], q[b]], axis=1)], axis=0).astype(BF16)
            kk_qk = _dot_nt(lhs, blockdiag2(k_b[a], k_b[b]))
            low.append(jnp.where(tri_strict, kk_qk[:CHUNK] * decay[p], 0.0))
            attn.append((kk_qk[CHUNK:] * decay[p]).astype(BF16))

        inv = [eye_pair - l for l in low]
        parts = [split_bd(l) for l in low]
        for _ in range(int(math.log2(CHUNK)) - 1):
            power = [pair_dot3(*pt) for pt in parts]
            parts = [split_bd(pw) for pw in power]
            inv = [t + pair_dot3(*_split2(t), pt[2], pt[3]) for t, pt in zip(inv, parts)]

        u, w = [None] * HEADS, [None] * HEADS
        for p, (a, b) in enumerate(pairs):
            xa_hi, xa_lo = _split2(jnp.concatenate([vb[a], kb[a] * e_gc[a]], axis=1))
            xb_hi, xb_lo = _split2(jnp.concatenate([vb[b], kb[b] * e_gc[b]], axis=1))
            x_hi = blockdiag2(xa_hi, xb_hi)
            t_hi, t_lo = _split2(inv[p])
            sol = _dot(jnp.concatenate([t_hi, t_hi, t_lo], axis=1),
                       jnp.concatenate([x_hi, blockdiag2(xa_lo, xb_lo), x_hi], axis=0))
            u[a], w[a], u[b], w[b] = (sol[:, n * HEAD_DIM:(n + 1) * HEAD_DIM] for n in range(4))

        state = [state_ref[h] for h in hs]
        ws_qs = [_dot(jnp.concatenate([w[h], q[h] * e_gc[h]], axis=0).astype(BF16), state[h].astype(BF16))
                 for h in hs]
        v_new = [(u[h] - ws_qs[h][:CHUNK]).astype(BF16) for h in hs]
        o = [None] * HEADS
        for p, (a, b) in enumerate(pairs):
            o_pair = _dot(attn[p], blockdiag2(v_new[a], v_new[b]))
            o[a] = ws_qs[a][CHUNK:] + o_pair[:, :HEAD_DIM]
            o[b] = ws_qs[b][CHUNK:] + o_pair[:, HEAD_DIM:]
        for h in hs:
            state_ref[h] = state[h] * jnp.exp(g_last[h]) + _dot_tn((k[h] * e_tail[h]).astype(BF16), v_new[h])
        for h in hs:
            gate = gate_ref[rows, cols[h]].astype(F32)
            on = o[h] * lax.rsqrt(jnp.mean(o[h] * o[h], axis=-1, keepdims=True) + EPS) * gn * _silu(gate)
            o_ref[rows, cols[h]] = on.astype(BF16)
        return carry

    lax.fori_loop(0, blk // CHUNK, chunk_body, 0)


def _gated_deltanet(main, small, conv_w, prm, gnorm, blk=512):
    t = main.shape[0]
    qcol = 3 * GROUP_W // GROUP_W
    big = lambda cb: pl.BlockSpec((blk, GROUP_W), lambda i: (i, cb))
    wspec = lambda cb: pl.BlockSpec((GDN_CONV, GROUP_W), lambda i: (0, cb))
    small_vec = lambda rows: pl.BlockSpec((rows, LANES), lambda i: (0, 0))
    return pl.pallas_call(
        functools.partial(_gdn_kernel, blk=blk),
        grid=(t // blk,),
        in_specs=[big(qcol), big(qcol + 1), big(qcol + 2), big(qcol + 3),
                  pl.BlockSpec((blk, LANES), lambda i: (i, 0)),
                  wspec(0), wspec(1), wspec(2),
                  small_vec(SUBLANES), small_vec(1)],
        out_specs=pl.BlockSpec((blk, GROUP_W), lambda i: (i, 0)),
        out_shape=jax.ShapeDtypeStruct((t, GROUP_W), BF16),
        scratch_shapes=[pltpu.VMEM((blk + SUBLANES, GROUP_W), F32)] * 3
                       + [pltpu.VMEM((blk, GROUP_W), F32)] * 3
                       + [pltpu.VMEM((HEADS, HEAD_DIM, HEAD_DIM), F32)],
        compiler_params=_cparams(("arbitrary",),
                                 5 * _nbytes((blk, GROUP_W), BF16) + _nbytes((blk, LANES), F32),
                                 6 * _nbytes((blk + SUBLANES, GROUP_W), F32)),
        name="gdn",
    )(main, main, main, main, small, conv_w, conv_w, conv_w, prm, gnorm)


def _outproj_kernel(oa_ref, ob_ref, w_ref, x_ref, g_ref, ga_ref, o_ref):
    y = _dot(oa_ref[...], w_ref[0:GROUP_W, :]) + _dot(ob_ref[...], w_ref[GROUP_W:2 * GROUP_W, :])
    yn = y * lax.rsqrt(jnp.mean(y * y, axis=-1, keepdims=True) + EPS) * g_ref[...]
    o_ref[...] = x_ref[...] + ga_ref[...] * yn


def _out_projection(o_a, o_b, w_out, x2, gain, gate, tm=512):
    t, d = x2.shape
    row = lambda i: (i, 0)
    vec = lambda i: (0, 0)
    return pl.pallas_call(
        _outproj_kernel,
        grid=(t // tm,),
        in_specs=[pl.BlockSpec((tm, GROUP_W), row), pl.BlockSpec((tm, GROUP_W), row),
                  pl.BlockSpec((2 * GROUP_W, d), vec),
                  pl.BlockSpec((tm, d), row),
                  pl.BlockSpec((1, d), vec), pl.BlockSpec((1, d), vec)],
        out_specs=pl.BlockSpec((tm, d), row),
        out_shape=jax.ShapeDtypeStruct((t, d), F32),
        compiler_params=_cparams(("arbitrary",),
                                 2 * _nbytes((tm, GROUP_W), BF16) + _nbytes((2 * GROUP_W, d), BF16)
                                 + 2 * _nbytes((tm, d), F32)),
        name="outproj",
    )(o_a, o_b, w_out, x2, gain, gate)


def _ffn_kernel(x_ref, g_ref, sc_ref, sh_ref, wg_ref, wv_ref, cwg_ref, cwv_ref, cbg_ref, cbv_ref, wd_ref,
                pg_ref, ga_ref, o_ref, h_ref, acc_ref, ug_ref, uv_ref, cg_ref, cv_ref, *, tm, nf):
    i = pl.program_id(0)
    j = pl.program_id(1)
    halo = SUBLANES

    @pl.when(j == 0)
    def _():
        h_ref[...] = _norm_modulate(x_ref[...], g_ref[...], sc_ref[...], sh_ref[...]).astype(BF16)
        acc_ref[...] = jnp.zeros_like(acc_ref)

    @pl.when(i == 0)
    def _():
        cg_ref[j] = jnp.zeros(cg_ref.shape[1:], F32)
        cv_ref[j] = jnp.zeros(cv_ref.shape[1:], F32)

    def up_conv(w_ref, u_ref, carry_ref, cw_ref, cb_ref):
        u = _dot(h_ref[...], w_ref[...])
        u_ref[0:halo, :] = carry_ref[j]
        u_ref[halo:halo + tm, :] = u
        carry_ref[j] = u[tm - halo:tm, :]
        y = cw_ref[FFN_CONV - 1:FFN_CONV, :] * u + cb_ref[...]
        for tap in range(FFN_CONV - 1):
            shift = FFN_CONV - 1 - tap
            y = y + cw_ref[tap:tap + 1, :] * u_ref[halo - shift:halo - shift + tm, :]
        return y

    gate = up_conv(wg_ref, ug_ref, cg_ref, cwg_ref, cbg_ref)
    val = up_conv(wv_ref, uv_ref, cv_ref, cwv_ref, cbv_ref)
    act = (_silu(gate) * val).astype(BF16)
    acc_ref[...] += _dot(act, wd_ref[...])

    @pl.when(j == nf - 1)
    def _():
        y = acc_ref[...]
        yn = y * lax.rsqrt(jnp.mean(y * y, axis=-1, keepdims=True) + EPS) * pg_ref[...]
        o_ref[...] = x_ref[...] + ga_ref[...] * yn


def _conv_ffn(x2, gain, scale, shift, w_up, conv_w, conv_b, w_down, post_gain, gate, tm=512, tf=512):
    t, d = x2.shape
    d_ff = w_down.shape[0]
    nf = d_ff // tf
    row = lambda i, j: (i, 0)
    vec = lambda i, j: (0, 0)
    gcol = lambda i, j: (0, j)
    vcol = lambda i, j: (0, nf + j)
    return pl.pallas_call(
        functools.partial(_ffn_kernel, tm=tm, nf=nf),
        grid=(t // tm, nf),
        in_specs=[pl.BlockSpec((tm, d), row),
                  pl.BlockSpec((1, d), vec), pl.BlockSpec((1, d), vec), pl.BlockSpec((1, d), vec),
                  pl.BlockSpec((d, tf), gcol), pl.BlockSpec((d, tf), vcol),
                  pl.BlockSpec((FFN_CONV, tf), gcol), pl.BlockSpec((FFN_CONV, tf), vcol),
                  pl.BlockSpec((1, tf), gcol), pl.BlockSpec((1, tf), vcol),
                  pl.BlockSpec((tf, d), lambda i, j: (j, 0)),
                  pl.BlockSpec((1, d), vec), pl.BlockSpec((1, d), vec)],
        out_specs=pl.BlockSpec((tm, d), row),
        out_shape=jax.ShapeDtypeStruct((t, d), F32),
        scratch_shapes=[pltpu.VMEM((tm, d), BF16), pltpu.VMEM((tm, d), F32),
                        pltpu.VMEM((tm + SUBLANES, tf), F32), pltpu.VMEM((tm + SUBLANES, tf), F32),
                        pltpu.VMEM((nf, SUBLANES, tf), F32), pltpu.VMEM((nf, SUBLANES, tf), F32)],
        compiler_params=_cparams(("arbitrary", "arbitrary"),
                                 2 * _nbytes((tm, d), F32) + 3 * _nbytes((d, tf), BF16),
                                 _nbytes((tm, d), BF16) + _nbytes((tm, d), F32)
                                 + 2 * _nbytes((tm + SUBLANES, tf), F32) + 2 * _nbytes((nf, SUBLANES, tf), F32)),
        name="ffn",
    )(x2, gain, scale, shift, w_up, w_up, conv_w, conv_w, conv_b, conv_b, w_down, post_gain, gate)


def _lane_row(vals, lane0):
    return jnp.zeros((1, LANES), F32).at[0, lane0:lane0 + HEADS].set(vals.astype(F32))


def _layer(x2, cond, p, attn_block):
    d = x2.shape[1]
    mod = _ada_mod(cond, p["w_ada"], p["b_ada"])
    sh_m, sc_m, ga_m, sh_f, sc_f, ga_f = (mod[:, k * d:(k + 1) * d] for k in range(6))

    w_in = p["w_in"]
    o_f = 3 * GROUP_W
    o_g = o_f + HEADS
    o_a = o_g + 3 * GROUP_W
    o_gate = o_a + 2 * HEADS
    w_main = jnp.concatenate([w_in[:, :o_f], w_in[:, o_g:o_a], w_in[:, o_gate:]], axis=1).astype(BF16)
    w_small = jnp.concatenate([w_in[:, o_f:o_g], w_in[:, o_a:o_gate],
                               jnp.zeros((d, LANES - 3 * HEADS), F32)], axis=1)

    main, small = _in_projection(x2, p["norm_mix_pre"].reshape(1, d), sc_m, sh_m, w_main, w_small)

    kx, cs = _fox_prologue(small, _lane_row(p["fox_forget_bias"], F_LANE), attn_block)
    nblk = x2.shape[0] // attn_block
    cs_flat = cs.reshape(nblk, SUBLANES, LANES)[:, 0, F_LANE:F_LANE + HEADS].T.reshape(-1)
    o_fox = _fox_attention(cs_flat, main, kx, p["fox_out_norm"].reshape(HEAD_DIM, 1), attn_block)

    prm = jnp.concatenate([_lane_row(p["gdn_dt_bias"], A_LANE), _lane_row(p["gdn_A_log"], A_LANE),
                           jnp.zeros((SUBLANES - 2, LANES), F32)], axis=0)
    o_gdn = _gated_deltanet(main, small, p["gdn_conv_w"], prm, p["gdn_out_norm"].reshape(1, HEAD_DIM))

    x2 = _out_projection(o_fox, o_gdn, p["w_out"].astype(BF16), x2, p["norm_mix_post"].reshape(1, d), ga_m)

    return _conv_ffn(x2, p["norm_ffn_pre"].reshape(1, d), sc_f, sh_f, p["w_up"].astype(BF16),
                     p["ffn_conv_w"], p["ffn_conv_b"].reshape(1, -1), p["w_down"].astype(BF16),
                     p["norm_ffn_post"].reshape(1, d), ga_f)


def kernel(x, c, w_ada, b_ada, norm_mix_pre, norm_mix_post, w_in, fox_forget_bias, fox_out_norm, gdn_conv_w, gdn_A_log, gdn_dt_bias, gdn_out_norm, w_out, norm_ffn_pre, norm_ffn_post, w_up, ffn_conv_w, ffn_conv_b, w_down):
    b, t, d = x.shape
    assert b == 1, "single-sequence kernel"
    params = dict(w_ada=w_ada, b_ada=b_ada, norm_mix_pre=norm_mix_pre, norm_mix_post=norm_mix_post, w_in=w_in,
                  fox_forget_bias=fox_forget_bias, fox_out_norm=fox_out_norm, gdn_conv_w=gdn_conv_w,
                  gdn_A_log=gdn_A_log, gdn_dt_bias=gdn_dt_bias, gdn_out_norm=gdn_out_norm, w_out=w_out,
                  norm_ffn_pre=norm_ffn_pre, norm_ffn_post=norm_ffn_post, w_up=w_up, ffn_conv_w=ffn_conv_w,
                  ffn_conv_b=ffn_conv_b, w_down=w_down)
    attn_block = min(512, t)
    x2 = x.reshape(t, d)
    for l in range(w_ada.shape[0]):
        x2 = _layer(x2, c, {k: v[l] for k, v in params.items()}, attn_block)
    return x2.reshape(b, t, d)
```

```python
import functools
import math

import jax
import jax.numpy as jnp
from jax import lax
from jax.experimental import pallas as pl
from jax.experimental.pallas import tpu as pltpu

F32 = jnp.float32
BF16 = jnp.bfloat16

EPS = 1e-6
HEADS = 8
HEAD_DIM = 128
CHUNK = 64
CHUNKS_PER_STEP = 4
GDN_CONV = 4
FFN_CONV = 3
LANES = 128
SUBLANES = 8
GROUP_W = HEADS * HEAD_DIM
LOG2E = math.log2(math.e)
NEG_INF = float("-inf")

F_LANE, A_LANE, B_LANE = 0, HEADS, 2 * HEADS


MIB = 1024 * 1024
V7X_VMEM_BYTES = 64 * MIB
TEMP_ALLOWANCE_BYTES = 22 * MIB


def _nbytes(shape, dtype):
    return math.prod(shape) * jnp.dtype(dtype).itemsize


def _cparams(semantics, window_bytes, scratch_bytes=0):
    want = 2 * window_bytes + scratch_bytes + TEMP_ALLOWANCE_BYTES
    return pltpu.CompilerParams(dimension_semantics=semantics,
                                vmem_limit_bytes=min(want, V7X_VMEM_BYTES - 4 * MIB))


def _sigmoid(x):
    return 0.5 * jnp.tanh(0.5 * x) + 0.5


def _silu(x):
    h = 0.5 * x
    return h + h * jnp.tanh(h)


def _split2(a):
    hi = a.astype(BF16)
    lo = (a - hi.astype(F32)).astype(BF16)
    return hi, lo


def _split3(a):
    hi = a.astype(BF16)
    r = a - hi.astype(F32)
    mid = r.astype(BF16)
    lo = (r - mid.astype(F32)).astype(BF16)
    return hi, mid, lo


def _dot(a, b):
    return jnp.dot(a, b, preferred_element_type=F32)


def _dot_nt(a, b):
    return lax.dot_general(a, b, (((1,), (1,)), ((), ())), preferred_element_type=F32)


def _dot_tn(a, b):
    return lax.dot_general(a, b, (((0,), (0,)), ((), ())), preferred_element_type=F32)


def _ada_kernel(c_ref, w_ref, b_ref, o_ref):
    cond = _silu(c_ref[...])
    o_ref[...] = jnp.sum(cond * w_ref[...], axis=0, keepdims=True) + b_ref[...]


def _ada_mod(c, w_ada, b_ada, tn=1024):
    d, n = w_ada.shape
    return pl.pallas_call(
        _ada_kernel,
        grid=(n // tn,),
        in_specs=[pl.BlockSpec((d, 1), lambda j: (0, 0)),
                  pl.BlockSpec((d, tn), lambda j: (0, j)),
                  pl.BlockSpec((1, tn), lambda j: (0, j))],
        out_specs=pl.BlockSpec((1, tn), lambda j: (0, j)),
        out_shape=jax.ShapeDtypeStruct((1, n), F32),
        compiler_params=_cparams(("arbitrary",), _nbytes((d, tn), F32)),
        name="ada",
    )(c.reshape(d, 1), w_ada, b_ada.reshape(1, n))


def _norm_modulate(x, gain, scale, shift):
    y = x * lax.rsqrt(jnp.mean(x * x, axis=-1, keepdims=True) + EPS) * gain
    return y * (1.0 + scale) + shift


def _inproj_kernel(x_ref, g_ref, sc_ref, sh_ref, wm_ref, ws_ref, om_ref, os_ref, h_ref, *, q_scale):
    j = pl.program_id(1)

    @pl.when(j == 0)
    def _():
        h = _norm_modulate(x_ref[...], g_ref[...], sc_ref[...], sh_ref[...])
        h_hi, h_lo = _split2(h)
        h_ref[...] = h_hi
        w_hi, w_lo = _split2(ws_ref[...])
        os_ref[...] = _dot(h_hi, w_hi) + _dot(h_hi, w_lo) + _dot(h_lo, w_hi)

    acc = _dot(h_ref[...], wm_ref[...])
    om_ref[...] = (acc * jnp.where(j == 0, q_scale, 1.0)).astype(BF16)


def _in_projection(x2, gain, scale, shift, w_main, w_small, tm=1024, tn=1024):
    t, d = x2.shape
    n = w_main.shape[1]
    row = lambda i, j: (i, 0)
    vec = lambda i, j: (0, 0)
    q_scale = HEAD_DIM ** -0.5 * LOG2E
    return pl.pallas_call(
        functools.partial(_inproj_kernel, q_scale=q_scale),
        grid=(t // tm, n // tn),
        in_specs=[pl.BlockSpec((tm, d), row),
                  pl.BlockSpec((1, d), vec), pl.BlockSpec((1, d), vec), pl.BlockSpec((1, d), vec),
                  pl.BlockSpec((d, tn), lambda i, j: (0, j)),
                  pl.BlockSpec((d, LANES), vec)],
        out_specs=[pl.BlockSpec((tm, tn), lambda i, j: (i, j)),
                   pl.BlockSpec((tm, LANES), row)],
        out_shape=[jax.ShapeDtypeStruct((t, n), BF16), jax.ShapeDtypeStruct((t, LANES), F32)],
        scratch_shapes=[pltpu.VMEM((tm, d), BF16)],
        compiler_params=_cparams(("arbitrary", "arbitrary"),
                                 _nbytes((tm, d), F32) + _nbytes((d, tn), BF16) + _nbytes((d, LANES), F32)
                                 + _nbytes((tm, tn), BF16) + _nbytes((tm, LANES), F32),
                                 _nbytes((tm, d), BF16)),
        name="inproj",
    )(x2, gain, scale, shift, w_main, w_small)


def _foxpre_kernel(s_ref, fb_ref, kx_ref, cs_ref, carry_ref, *, tb):
    i = pl.program_id(0)

    @pl.when(i == 0)
    def _():
        carry_ref[...] = jnp.zeros_like(carry_ref)

    z = s_ref[...] + fb_ref[...]
    log_f = jnp.minimum(z, 0.0) - jnp.log1p(jnp.exp(-jnp.abs(z)))
    r = lax.broadcasted_iota(jnp.int32, (tb, tb), 0)
    c = lax.broadcasted_iota(jnp.int32, (tb, tb), 1)
    tri = jnp.where(c <= r, 1.0, 0.0).astype(BF16)
    hi, mid, lo = _split3(log_f)
    cum3 = _dot(tri, jnp.concatenate([hi, mid, lo], axis=1))
    cum = cum3[:, :LANES] + cum3[:, LANES:2 * LANES] + cum3[:, 2 * LANES:]
    carry = carry_ref[...]
    cs_ref[...] = jnp.broadcast_to(carry * LOG2E, cs_ref.shape)
    carry_ref[...] = carry + cum[tb - 1:tb, :]

    dd = -cum * LOG2E
    lane = lax.broadcasted_iota(jnp.int32, (tb, HEAD_DIM), 1)
    for h in range(HEADS):
        col = jnp.broadcast_to(dd[:, F_LANE + h:F_LANE + h + 1], (tb, HEAD_DIM))
        c_hi = col.astype(BF16).astype(F32)
        c_mid = (col - c_hi).astype(BF16).astype(F32)
        c_lo = col - c_hi - c_mid
        kx = jnp.where(lane == 0, c_hi, jnp.where(lane == 1, c_mid, jnp.where(lane == 2, c_lo, 0.0)))
        kx_ref[:, h * HEAD_DIM:(h + 1) * HEAD_DIM] = kx.astype(BF16)


def _fox_prologue(small, fbias_row, tb):
    t = small.shape[0]
    nblk = t // tb
    return pl.pallas_call(
        functools.partial(_foxpre_kernel, tb=tb),
        grid=(nblk,),
        in_specs=[pl.BlockSpec((tb, LANES), lambda i: (i, 0)),
                  pl.BlockSpec((1, LANES), lambda i: (0, 0))],
        out_specs=[pl.BlockSpec((tb, GROUP_W), lambda i: (i, 0)),
                   pl.BlockSpec((SUBLANES, LANES), lambda i: (i, 0))],
        out_shape=[jax.ShapeDtypeStruct((t, GROUP_W), BF16),
                   jax.ShapeDtypeStruct((nblk * SUBLANES, LANES), F32)],
        scratch_shapes=[pltpu.VMEM((1, LANES), F32)],
        compiler_params=_cparams(("arbitrary",), _nbytes((tb, LANES), F32) + _nbytes((tb, GROUP_W), BF16)),
        name="foxpre",
    )(small, fbias_row)


V_ROWS = HEAD_DIM + 16


def _fox_kernel(cs_ref, q_ref, k_ref, v_ref, kx_ref, gn_ref, o_ref,
                vt_ref, qt_ref, sa_ref, sb_ref, m_ref, acc_ref, *, tk, nblk):
    tq = 2 * tk
    h = pl.program_id(0)
    i = pl.program_id(1)

    @pl.when(i == 0)
    def _():
        row = lax.broadcasted_iota(jnp.int32, (V_ROWS - HEAD_DIM, tk), 0)
        tail = jnp.where(row == 0, 1.0, 0.0).astype(BF16)

        def fill(j, carry):
            off = pl.multiple_of(j * tk, tk)
            vt = v_ref[pl.ds(off, tk), :].astype(F32).T.astype(BF16)
            vt_ref[j] = jnp.concatenate([vt, tail], axis=0)
            return carry

        lax.fori_loop(0, nblk, fill, 0)

    row = lax.broadcasted_iota(jnp.int32, (HEAD_DIM, tq), 0)
    qt_ref[0:HEAD_DIM, :] = q_ref[...].astype(F32).T.astype(BF16)
    qt_ref[HEAD_DIM:2 * HEAD_DIM, :] = jnp.where(row < 3, 1.0, 0.0).astype(BF16)
    m_ref[...] = jnp.full_like(m_ref, NEG_INF)
    acc_ref[...] = jnp.zeros_like(acc_ref)
    first = 2 * i
    cq = cs_ref[h * nblk + first]

    def scores(j, s_ref):
        off = pl.multiple_of(j * tk, tk)
        k_aug = jnp.concatenate([k_ref[pl.ds(off, tk), :], kx_ref[pl.ds(off, tk), :]], axis=1)
        s_ref[...] = _dot(k_aug, qt_ref[...])

    def softmax_pv(j, s_ref, key0=None):
        s = s_ref[...]
        if key0 is not None:
            key = lax.broadcasted_iota(jnp.int32, (tk, tq), 0) + key0
            qry = lax.broadcasted_iota(jnp.int32, (tk, tq), 1)
            s = jnp.where(key <= qry, s, NEG_INF)
        delta = cq - cs_ref[h * nblk + j]
        m_old = m_ref[...]
        m_new = jnp.maximum(m_old, jnp.max(s, axis=0, keepdims=True) + delta)
        p = jnp.exp2(s - (m_new - delta))
        acc_ref[...] = acc_ref[...] * jnp.exp2(m_old - m_new) + _dot(vt_ref[j], p.astype(BF16))
        m_ref[...] = m_new

    scores(0, sa_ref)

    def pair_body(jj, carry):
        j0 = 2 * jj
        scores(j0 + 1, sb_ref)
        softmax_pv(j0, sa_ref)
        scores(j0 + 2, sa_ref)
        softmax_pv(j0 + 1, sb_ref)
        return carry

    lax.fori_loop(0, i, pair_body, 0)
    scores(first + 1, sb_ref)
    softmax_pv(first, sa_ref, key0=0)
    softmax_pv(first + 1, sb_ref, key0=tk)

    acc = acc_ref[...]
    o = acc[:HEAD_DIM, :] / acc[HEAD_DIM:HEAD_DIM + 1, :]
    o = o * lax.rsqrt(jnp.mean(o * o, axis=0, keepdims=True) + EPS) * gn_ref[...]
    o_ref[...] = o.T.astype(BF16)


def _fox_attention(cs_flat, main, kx, gnorm_col, tk):
    t = main.shape[0]
    tq = 2 * tk
    nblk = t // tk
    grid_spec = pltpu.PrefetchScalarGridSpec(
        num_scalar_prefetch=1,
        grid=(HEADS, t // tq),
        in_specs=[pl.BlockSpec((tq, HEAD_DIM), lambda h, i, cs: (i, h)),
                  pl.BlockSpec((t, HEAD_DIM), lambda h, i, cs: (0, HEADS + h)),
                  pl.BlockSpec((t, HEAD_DIM), lambda h, i, cs: (0, 2 * HEADS + h)),
                  pl.BlockSpec((t, HEAD_DIM), lambda h, i, cs: (0, h)),
                  pl.BlockSpec((HEAD_DIM, 1), lambda h, i, cs: (0, 0))],
        out_specs=pl.BlockSpec((tq, HEAD_DIM), lambda h, i, cs: (i, h)),
        scratch_shapes=[pltpu.VMEM((nblk, V_ROWS, tk), BF16), pltpu.VMEM((2 * HEAD_DIM, tq), BF16),
                        pltpu.VMEM((tk, tq), F32), pltpu.VMEM((tk, tq), F32),
                        pltpu.VMEM((1, tq), F32), pltpu.VMEM((V_ROWS, tq), F32)],
    )
    return pl.pallas_call(
        functools.partial(_fox_kernel, tk=tk, nblk=nblk),
        grid_spec=grid_spec,
        out_shape=jax.ShapeDtypeStruct((t, GROUP_W), BF16),
        compiler_params=_cparams(("arbitrary", "arbitrary"),
                                 3 * _nbytes((t, HEAD_DIM), BF16) + 2 * _nbytes((tq, HEAD_DIM), BF16),
                                 _nbytes((nblk, V_ROWS, tk), BF16) + _nbytes((2 * HEAD_DIM, tq), BF16)
                                 + 2 * _nbytes((tk, tq), F32) + _nbytes((V_ROWS + SUBLANES, tq), F32)),
        name="fox",
    )(cs_flat, main, main, main, kx, gnorm_col)


def _gdn_kernel(q_ref, k_ref, v_ref, gate_ref, s_ref, wq_ref, wk_ref, wv_ref, prm_ref, gn_ref, o_ref,
                xq_ref, xk_ref, xv_ref, yq_ref, yk_ref, yv_ref, state_ref, *, blk):
    i = pl.program_id(0)
    halo = SUBLANES

    @pl.when(i == 0)
    def _():
        state_ref[...] = jnp.zeros_like(state_ref)
        for x_ref in (xq_ref, xk_ref, xv_ref):
            x_ref[0:halo, :] = jnp.zeros((halo, GROUP_W), F32)

    def conv_silu(src_ref, x_ref, w_ref):
        x_ref[halo:halo + blk, :] = src_ref[...].astype(F32)
        y = w_ref[GDN_CONV - 1:GDN_CONV, :] * x_ref[halo:halo + blk, :]
        for tap in range(GDN_CONV - 1):
            shift = GDN_CONV - 1 - tap
            y = y + w_ref[tap:tap + 1, :] * x_ref[halo - shift:halo - shift + blk, :]
        x_ref[0:halo, :] = x_ref[blk:blk + halo, :]
        return _silu(y)

    def l2norm_heads(y, dst_ref, mult):
        for h in range(HEADS):
            yh = y[:, h * HEAD_DIM:(h + 1) * HEAD_DIM]
            inv = lax.rsqrt(jnp.sum(yh * yh, axis=-1, keepdims=True) + EPS) * mult
            dst_ref[:, h * HEAD_DIM:(h + 1) * HEAD_DIM] = yh * inv

    l2norm_heads(conv_silu(q_ref, xq_ref, wq_ref), yq_ref, HEAD_DIM ** -0.5)
    l2norm_heads(conv_silu(k_ref, xk_ref, wk_ref), yk_ref, 1.0)
    yv_ref[...] = conv_silu(v_ref, xv_ref, wv_ref)

    row = lax.broadcasted_iota(jnp.int32, (CHUNK, LANES), 0)
    lane = lax.broadcasted_iota(jnp.int32, (CHUNK, LANES), 1)
    left = lane < CHUNK
    lane_in = jnp.where(left, lane, lane - CHUNK)
    tri_incl = lane_in <= row
    tri_strict = lane_in < row
    eye_pair = jnp.where(lane_in == row, 1.0, 0.0)
    m_left = jnp.where(left, 1.0, 0.0).astype(BF16)
    m_right = jnp.where(left, 0.0, 1.0).astype(BF16)
    left_row = left[0:1, :]
    r2 = lax.broadcasted_iota(jnp.int32, (2 * CHUNK, CHUNK), 0)
    c2 = lax.broadcasted_iota(jnp.int32, (2 * CHUNK, CHUNK), 1)
    tri_twice = jnp.where(c2 <= jnp.where(r2 < CHUNK, r2, r2 - CHUNK), 1.0, 0.0).astype(BF16)
    dt_bias = prm_ref[0:1, :]
    neg_a = -jnp.exp(prm_ref[1:2, :])
    gn = gn_ref[...]

    def blockdiag(x_b):
        return jnp.concatenate([x_b * m_left, x_b * m_right], axis=0)

    def split_bd(x):
        hi, lo = _split2(x)
        return hi, lo, blockdiag(hi), blockdiag(lo)

    def pair_dot3(a_hi, a_lo, b_hi_bd, b_lo_bd):
        return _dot(jnp.concatenate([a_hi, a_hi, a_lo], axis=1),
                    jnp.concatenate([b_hi_bd, b_lo_bd, b_hi_bd], axis=0))

    def blockdiag2(a, b):
        z = jnp.zeros_like(a)
        return jnp.concatenate([jnp.concatenate([a, z], axis=1), jnp.concatenate([z, b], axis=1)], axis=0)

    def group_body(gi, carry):
        nc = CHUNKS_PER_STEP
        rows = [pl.ds(pl.multiple_of((gi * nc + c) * CHUNK, CHUNK), CHUNK) for c in range(nc)]
        gc, gc2_t, beta = [], [], []
        for c in range(nc):
            slab = s_ref[rows[c], :]
            z = slab + dt_bias
            g = neg_a * (jnp.maximum(z, 0.0) + jnp.log1p(jnp.exp(-jnp.abs(z))))
            beta.append(_sigmoid(slab))
            g_hi, g_mid, g_lo = _split3(g)
            gc3 = _dot(tri_twice, jnp.concatenate([g_hi, g_mid, g_lo], axis=1))
            gc2 = gc3[:, :LANES] + gc3[:, LANES:2 * LANES] + gc3[:, 2 * LANES:]
            gc.append(gc2[:CHUNK])
            gc2_t.append(gc2.T)

        slots = [(c, h) for c in range(nc) for h in range(HEADS)]
        cols = [slice(h * HEAD_DIM, (h + 1) * HEAD_DIM) for h in range(HEADS)]
        gcol = [gc[c][:, A_LANE + h:A_LANE + h + 1] for c, h in slots]
        g_last = [gc[c][CHUNK - 1:CHUNK, A_LANE + h:A_LANE + h + 1] for c, h in slots]
        e_gc = [jnp.exp(x) for x in gcol]
        e_tail = [jnp.exp(gl - x) for gl, x in zip(g_last, gcol)]
        bcol = [beta[c][:, B_LANE + h:B_LANE + h + 1] for c, h in slots]
        q = [yq_ref[rows[c], cols[h]] for c, h in slots]
        k = [yk_ref[rows[c], cols[h]] for c, h in slots]
        kb = [x * b for x, b in zip(k, bcol)]
        vb = [yv_ref[rows[c], cols[h]] * bcol[c * HEADS + h] for c, h in slots]
        k_b = [x.astype(BF16) for x in k]
        pairs = [(c, c * HEADS + 2 * p, c * HEADS + 2 * p + 1) for c in range(nc) for p in range(HEADS // 2)]

        decay, low, attn = [], [], []
        for c, a, b in pairs:
            ha, hb = a - c * HEADS, b - c * HEADS
            diff = (jnp.where(left, gcol[a], gcol[b])
                    - jnp.where(left_row, gc2_t[c][A_LANE + ha:A_LANE + ha + 1, :],
                                gc2_t[c][A_LANE + hb:A_LANE + hb + 1, :]))
            decay.append(jnp.exp(jnp.where(tri_incl, diff, NEG_INF)))
        for p, (c, a, b) in enumerate(pairs):
            lhs = jnp.concatenate([jnp.concatenate([kb[a], kb[b]], axis=1),
                                   jnp.concatenate([q[a], q[b]], axis=1)], axis=0).astype(BF16)
            kk_qk = _dot_nt(lhs, blockdiag2(k_b[a], k_b[b]))
            low.append(jnp.where(tri_strict, kk_qk[:CHUNK] * decay[p], 0.0))
            attn.append((kk_qk[CHUNK:] * decay[p]).astype(BF16))

        power = [pair_dot3(*split_bd(-l)) for l in low]
        inv = [eye_pair - l for l in low]
        for _ in range(int(math.log2(CHUNK)) - 2):
            both = []
            for s, pw in zip(inv, power):
                hi, lo = _split2(jnp.concatenate([s, pw], axis=0))
                both.append(pair_dot3(hi, lo, blockdiag(hi[CHUNK:]), blockdiag(lo[CHUNK:])))
            inv = [s + r[:CHUNK] for s, r in zip(inv, both)]
            power = [r[CHUNK:] for r in both]
        inv = [s + pair_dot3(*_split2(s), *split_bd(pw)[2:]) for s, pw in zip(inv, power)]

        u, w = [None] * len(slots), [None] * len(slots)
        for p, (c, a, b) in enumerate(pairs):
            xa_hi, xa_lo = _split2(jnp.concatenate([vb[a], kb[a] * e_gc[a]], axis=1))
            xb_hi, xb_lo = _split2(jnp.concatenate([vb[b], kb[b] * e_gc[b]], axis=1))
            x_hi = blockdiag2(xa_hi, xb_hi)
            t_hi, t_lo = _split2(inv[p])
            sol = _dot(jnp.concatenate([t_hi, t_hi, t_lo], axis=1),
                       jnp.concatenate([x_hi, blockdiag2(xa_lo, xb_lo), x_hi], axis=0))
            u[a], w[a], u[b], w[b] = (sol[:, n * HEAD_DIM:(n + 1) * HEAD_DIM] for n in range(4))
        wq = [jnp.concatenate([w[s], q[s] * e_gc[s]], axis=0).astype(BF16) for s in range(len(slots))]
        k_tail = [(k[s] * e_tail[s]).astype(BF16) for s in range(len(slots))]
        e_last = [jnp.exp(x) for x in g_last]

        for c in range(nc):
            base = c * HEADS
            state = [state_ref[h] for h in range(HEADS)]
            ws_qs = [_dot(wq[base + h], state[h].astype(BF16)) for h in range(HEADS)]
            v_new = [(u[base + h] - ws_qs[h][:CHUNK]).astype(BF16) for h in range(HEADS)]
            for h in range(HEADS):
                state_ref[h] = state[h] * e_last[base + h] + _dot_tn(k_tail[base + h], v_new[h])
            for p in range(HEADS // 2):
                a, b = 2 * p, 2 * p + 1
                o_pair = _dot(attn[c * (HEADS // 2) + p], blockdiag2(v_new[a], v_new[b]))
                for h, o_mm in ((a, o_pair[:, :HEAD_DIM]), (b, o_pair[:, HEAD_DIM:])):
                    o = ws_qs[h][CHUNK:] + o_mm
                    gate = gate_ref[rows[c], cols[h]].astype(F32)
                    on = o * lax.rsqrt(jnp.mean(o * o, axis=-1, keepdims=True) + EPS) * gn * _silu(gate)
                    o_ref[rows[c], cols[h]] = on.astype(BF16)
        return carry

    lax.fori_loop(0, blk // (CHUNK * CHUNKS_PER_STEP), group_body, 0)


def _gated_deltanet(main, small, conv_w, prm, gnorm, blk=512):
    t = main.shape[0]
    qcol = 3 * GROUP_W // GROUP_W
    big = lambda cb: pl.BlockSpec((blk, GROUP_W), lambda i: (i, cb))
    wspec = lambda cb: pl.BlockSpec((GDN_CONV, GROUP_W), lambda i: (0, cb))
    small_vec = lambda rows: pl.BlockSpec((rows, LANES), lambda i: (0, 0))
    return pl.pallas_call(
        functools.partial(_gdn_kernel, blk=blk),
        grid=(t // blk,),
        in_specs=[big(qcol), big(qcol + 1), big(qcol + 2), big(qcol + 3),
                  pl.BlockSpec((blk, LANES), lambda i: (i, 0)),
                  wspec(0), wspec(1), wspec(2),
                  small_vec(SUBLANES), small_vec(1)],
        out_specs=pl.BlockSpec((blk, GROUP_W), lambda i: (i, 0)),
        out_shape=jax.ShapeDtypeStruct((t, GROUP_W), BF16),
        scratch_shapes=[pltpu.VMEM((blk + SUBLANES, GROUP_W), F32)] * 3
                       + [pltpu.VMEM((blk, GROUP_W), F32)] * 3
                       + [pltpu.VMEM((HEADS, HEAD_DIM, HEAD_DIM), F32)],
        compiler_params=_cparams(("arbitrary",),
                                 5 * _nbytes((blk, GROUP_W), BF16) + _nbytes((blk, LANES), F32),
                                 6 * _nbytes((blk + SUBLANES, GROUP_W), F32)),
        name="gdn",
    )(main, main, main, main, small, conv_w, conv_w, conv_w, prm, gnorm)


def _outproj_kernel(oa_ref, ob_ref, w_ref, x_ref, g_ref, ga_ref, o_ref):
    y = _dot(oa_ref[...], w_ref[0:GROUP_W, :]) + _dot(ob_ref[...], w_ref[GROUP_W:2 * GROUP_W, :])
    yn = y * lax.rsqrt(jnp.mean(y * y, axis=-1, keepdims=True) + EPS) * g_ref[...]
    o_ref[...] = x_ref[...] + ga_ref[...] * yn


def _out_projection(o_a, o_b, w_out, x2, gain, gate, tm=512):
    t, d = x2.shape
    row = lambda i: (i, 0)
    vec = lambda i: (0, 0)
    return pl.pallas_call(
        _outproj_kernel,
        grid=(t // tm,),
        in_specs=[pl.BlockSpec((tm, GROUP_W), row), pl.BlockSpec((tm, GROUP_W), row),
                  pl.BlockSpec((2 * GROUP_W, d), vec),
                  pl.BlockSpec((tm, d), row),
                  pl.BlockSpec((1, d), vec), pl.BlockSpec((1, d), vec)],
        out_specs=pl.BlockSpec((tm, d), row),
        out_shape=jax.ShapeDtypeStruct((t, d), F32),
        compiler_params=_cparams(("arbitrary",),
                                 2 * _nbytes((tm, GROUP_W), BF16) + _nbytes((2 * GROUP_W, d), BF16)
                                 + 2 * _nbytes((tm, d), F32)),
        name="outproj",
    )(o_a, o_b, w_out, x2, gain, gate)


def _ffn_kernel(x_ref, g_ref, sc_ref, sh_ref, wg_ref, wv_ref, cwg_ref, cwv_ref, cbg_ref, cbv_ref, wd_ref,
                pg_ref, ga_ref, o_ref, h_ref, acc_ref, ug_ref, uv_ref, cg_ref, cv_ref, *, tm, nf):
    i = pl.program_id(0)
    j = pl.program_id(1)
    halo = SUBLANES

    @pl.when(j == 0)
    def _():
        h_ref[...] = _norm_modulate(x_ref[...], g_ref[...], sc_ref[...], sh_ref[...]).astype(BF16)
        acc_ref[...] = jnp.zeros_like(acc_ref)

    @pl.when(i == 0)
    def _():
        cg_ref[j] = jnp.zeros(cg_ref.shape[1:], F32)
        cv_ref[j] = jnp.zeros(cv_ref.shape[1:], F32)

    def up_conv(w_ref, u_ref, carry_ref, cw_ref, cb_ref):
        u = _dot(h_ref[...], w_ref[...])
        u_ref[0:halo, :] = carry_ref[j]
        u_ref[halo:halo + tm, :] = u
        carry_ref[j] = u[tm - halo:tm, :]
        y = cw_ref[FFN_CONV - 1:FFN_CONV, :] * u + cb_ref[...]
        for tap in range(FFN_CONV - 1):
            shift = FFN_CONV - 1 - tap
            y = y + cw_ref[tap:tap + 1, :] * u_ref[halo - shift:halo - shift + tm, :]
        return y

    gate = up_conv(wg_ref, ug_ref, cg_ref, cwg_ref, cbg_ref)
    val = up_conv(wv_ref, uv_ref, cv_ref, cwv_ref, cbv_ref)
    act = (_silu(gate) * val).astype(BF16)
    acc_ref[...] += _dot(act, wd_ref[...])

    @pl.when(j == nf - 1)
    def _():
        y = acc_ref[...]
        yn = y * lax.rsqrt(jnp.mean(y * y, axis=-1, keepdims=True) + EPS) * pg_ref[...]
        o_ref[...] = x_ref[...] + ga_ref[...] * yn


def _conv_ffn(x2, gain, scale, shift, w_up, conv_w, conv_b, w_down, post_gain, gate, tm=512, tf=512):
    t, d = x2.shape
    d_ff = w_down.shape[0]
    nf = d_ff // tf
    row = lambda i, j: (i, 0)
    vec = lambda i, j: (0, 0)
    gcol = lambda i, j: (0, j)
    vcol = lambda i, j: (0, nf + j)
    return pl.pallas_call(
        functools.partial(_ffn_kernel, tm=tm, nf=nf),
        grid=(t // tm, nf),
        in_specs=[pl.BlockSpec((tm, d), row),
                  pl.BlockSpec((1, d), vec), pl.BlockSpec((1, d), vec), pl.BlockSpec((1, d), vec),
                  pl.BlockSpec((d, tf), gcol), pl.BlockSpec((d, tf), vcol),
                  pl.BlockSpec((FFN_CONV, tf), gcol), pl.BlockSpec((FFN_CONV, tf), vcol),
                  pl.BlockSpec((1, tf), gcol), pl.BlockSpec((1, tf), vcol),
                  pl.BlockSpec((tf, d), lambda i, j: (j, 0)),
                  pl.BlockSpec((1, d), vec), pl.BlockSpec((1, d), vec)],
        out_specs=pl.BlockSpec((tm, d), row),
        out_shape=jax.ShapeDtypeStruct((t, d), F32),
        scratch_shapes=[pltpu.VMEM((tm, d), BF16), pltpu.VMEM((tm, d), F32),
                        pltpu.VMEM((tm + SUBLANES, tf), F32), pltpu.VMEM((tm + SUBLANES, tf), F32),
                        pltpu.VMEM((nf, SUBLANES, tf), F32), pltpu.VMEM((nf, SUBLANES, tf), F32)],
        compiler_params=_cparams(("arbitrary", "arbitrary"),
                                 2 * _nbytes((tm, d), F32) + 3 * _nbytes((d, tf), BF16),
                                 _nbytes((tm, d), BF16) + _nbytes((tm, d), F32)
                                 + 2 * _nbytes((tm + SUBLANES, tf), F32) + 2 * _nbytes((nf, SUBLANES, tf), F32)),
        name="ffn",
    )(x2, gain, scale, shift, w_up, w_up, conv_w, conv_w, conv_b, conv_b, w_down, post_gain, gate)


def _lane_row(vals, lane0):
    return jnp.zeros((1, LANES), F32).at[0, lane0:lane0 + HEADS].set(vals.astype(F32))


def _layer(x2, cond, p, attn_block):
    d = x2.shape[1]
    mod = _ada_mod(cond, p["w_ada"], p["b_ada"])
    sh_m, sc_m, ga_m, sh_f, sc_f, ga_f = (mod[:, k * d:(k + 1) * d] for k in range(6))

    w_in = p["w_in"]
    o_f = 3 * GROUP_W
    o_g = o_f + HEADS
    o_a = o_g + 3 * GROUP_W
    o_gate = o_a + 2 * HEADS
    w_main = jnp.concatenate([w_in[:, :o_f], w_in[:, o_g:o_a], w_in[:, o_gate:]], axis=1).astype(BF16)
    w_small = jnp.concatenate([w_in[:, o_f:o_g], w_in[:, o_a:o_gate],
                               jnp.zeros((d, LANES - 3 * HEADS), F32)], axis=1)

    main, small = _in_projection(x2, p["norm_mix_pre"].reshape(1, d), sc_m, sh_m, w_main, w_small)

    kx, cs = _fox_prologue(small, _lane_row(p["fox_forget_bias"], F_LANE), attn_block)
    nblk = x2.shape[0] // attn_block
    cs_flat = cs.reshape(nblk, SUBLANES, LANES)[:, 0, F_LANE:F_LANE + HEADS].T.reshape(-1)
    o_fox = _fox_attention(cs_flat, main, kx, p["fox_out_norm"].reshape(HEAD_DIM, 1), attn_block)

    prm = jnp.concatenate([_lane_row(p["gdn_dt_bias"], A_LANE), _lane_row(p["gdn_A_log"], A_LANE),
                           jnp.zeros((SUBLANES - 2, LANES), F32)], axis=0)
    o_gdn = _gated_deltanet(main, small, p["gdn_conv_w"], prm, p["gdn_out_norm"].reshape(1, HEAD_DIM))

    x2 = _out_projection(o_fox, o_gdn, p["w_out"].astype(BF16), x2, p["norm_mix_post"].reshape(1, d), ga_m)

    return _conv_ffn(x2, p["norm_ffn_pre"].reshape(1, d), sc_f, sh_f, p["w_up"].astype(BF16),
                     p["ffn_conv_w"], p["ffn_conv_b"].reshape(1, -1), p["w_down"].astype(BF16),
                     p["norm_ffn_post"].reshape(1, d), ga_f)


def kernel(x, c, w_ada, b_ada, norm_mix_pre, norm_mix_post, w_in, fox_forget_bias, fox_out_norm, gdn_conv_w, gdn_A_log, gdn_dt_bias, gdn_out_norm, w_out, norm_ffn_pre, norm_ffn_post, w_up, ffn_conv_w, ffn_conv_b, w_down):
    b, t, d = x.shape
    assert b == 1, "single-sequence kernel"
    params = dict(w_ada=w_ada, b_ada=b_ada, norm_mix_pre=norm_mix_pre, norm_mix_post=norm_mix_post, w_in=w_in,
                  fox_forget_bias=fox_forget_bias, fox_out_norm=fox_out_norm, gdn_conv_w=gdn_conv_w,
                  gdn_A_log=gdn_A_log, gdn_dt_bias=gdn_dt_bias, gdn_out_norm=gdn_out_norm, w_out=w_out,
                  norm_ffn_pre=norm_ffn_pre, norm_ffn_post=norm_ffn_post, w_up=w_up, ffn_conv_w=ffn_conv_w,
                  ffn_conv_b=ffn_conv_b, w_down=w_down)
    attn_block = min(512, t)
    x2 = x.reshape(t, d)
    for l in range(w_ada.shape[0]):
        x2 = _layer(x2, c, {k: v[l] for k, v in params.items()}, attn_block)
    return x2.reshape(b, t, d)
```

```python
import functools
import math

import jax
import jax.numpy as jnp
from jax import lax
from jax.experimental import pallas as pl
from jax.experimental.pallas import tpu as pltpu

F32 = jnp.float32
BF16 = jnp.bfloat16

EPS = 1e-6
HEADS = 8
HEAD_DIM = 128
CHUNK = 64
CHUNKS_PER_STEP = 4
GDN_CONV = 4
FFN_CONV = 3
LANES = 128
SUBLANES = 8
MXU_COLS = 256
GROUP_W = HEADS * HEAD_DIM
LOG2E = math.log2(math.e)
NEG_INF = float("-inf")

F_LANE, A_LANE, B_LANE = 0, HEADS, 2 * HEADS


MIB = 1024 * 1024
V7X_VMEM_BYTES = 64 * MIB
TEMP_ALLOWANCE_BYTES = 22 * MIB


def _nbytes(shape, dtype):
    return math.prod(shape) * jnp.dtype(dtype).itemsize


def _cparams(semantics, window_bytes, scratch_bytes=0, flags=None):
    want = 2 * window_bytes + scratch_bytes + TEMP_ALLOWANCE_BYTES
    return pltpu.CompilerParams(dimension_semantics=semantics, flags=flags,
                                vmem_limit_bytes=min(want, V7X_VMEM_BYTES - 4 * MIB))


def _sigmoid(x):
    return 0.5 * jnp.tanh(0.5 * x) + 0.5


def _silu(x):
    h = 0.5 * x
    return h + h * jnp.tanh(h)


def _split2(a):
    hi = a.astype(BF16)
    lo = (a - hi.astype(F32)).astype(BF16)
    return hi, lo


def _split3(a):
    hi = a.astype(BF16)
    r = a - hi.astype(F32)
    mid = r.astype(BF16)
    lo = (r - mid.astype(F32)).astype(BF16)
    return hi, mid, lo


def _dot(a, b):
    return jnp.dot(a, b, preferred_element_type=F32)


def _dot_nt(a, b):
    return lax.dot_general(a, b, (((1,), (1,)), ((), ())), preferred_element_type=F32)


def _dot_tn(a, b):
    return lax.dot_general(a, b, (((0,), (0,)), ((), ())), preferred_element_type=F32)


def _ada_kernel(c_ref, w_ref, b_ref, o_ref):
    cond = _silu(c_ref[...])
    o_ref[...] = jnp.sum(cond * w_ref[...], axis=0, keepdims=True) + b_ref[...]


def _ada_mod(c, w_ada, b_ada, tn=1024):
    d, n = w_ada.shape
    return pl.pallas_call(
        _ada_kernel,
        grid=(n // tn,),
        in_specs=[pl.BlockSpec((d, 1), lambda j: (0, 0)),
                  pl.BlockSpec((d, tn), lambda j: (0, j)),
                  pl.BlockSpec((1, tn), lambda j: (0, j))],
        out_specs=pl.BlockSpec((1, tn), lambda j: (0, j)),
        out_shape=jax.ShapeDtypeStruct((1, n), F32),
        compiler_params=_cparams(("arbitrary",), _nbytes((d, tn), F32)),
        name="ada",
    )(c.reshape(d, 1), w_ada, b_ada.reshape(1, n))


def _norm_modulate(x, gain, scale, shift):
    y = x * lax.rsqrt(jnp.mean(x * x, axis=-1, keepdims=True) + EPS) * gain
    return y * (1.0 + scale) + shift


def _inproj_kernel(x_ref, g_ref, sc_ref, sh_ref, wm_ref, ws_ref, om_ref, os_ref, h_ref, *, q_scale):
    j = pl.program_id(1)

    @pl.when(j == 0)
    def _():
        h = _norm_modulate(x_ref[...], g_ref[...], sc_ref[...], sh_ref[...])
        h_hi, h_lo = _split2(h)
        h_ref[...] = h_hi
        w_hi, w_lo = _split2(ws_ref[...])
        os_ref[...] = _dot(h_hi, w_hi) + _dot(h_hi, w_lo) + _dot(h_lo, w_hi)

    acc = _dot(h_ref[...], wm_ref[...])
    om_ref[...] = (acc * jnp.where(j == 0, q_scale, 1.0)).astype(BF16)


def _in_projection(x2, gain, scale, shift, w_main, w_small, tm=1024, tn=1024):
    t, d = x2.shape
    n = w_main.shape[1]
    row = lambda i, j: (i, 0)
    vec = lambda i, j: (0, 0)
    q_scale = HEAD_DIM ** -0.5 * LOG2E
    return pl.pallas_call(
        functools.partial(_inproj_kernel, q_scale=q_scale),
        grid=(t // tm, n // tn),
        in_specs=[pl.BlockSpec((tm, d), row),
                  pl.BlockSpec((1, d), vec), pl.BlockSpec((1, d), vec), pl.BlockSpec((1, d), vec),
                  pl.BlockSpec((d, tn), lambda i, j: (0, j)),
                  pl.BlockSpec((d, LANES), vec)],
        out_specs=[pl.BlockSpec((tm, tn), lambda i, j: (i, j)),
                   pl.BlockSpec((tm, LANES), row)],
        out_shape=[jax.ShapeDtypeStruct((t, n), BF16), jax.ShapeDtypeStruct((t, LANES), F32)],
        scratch_shapes=[pltpu.VMEM((tm, d), BF16)],
        compiler_params=_cparams(("arbitrary", "arbitrary"),
                                 _nbytes((tm, d), F32) + _nbytes((d, tn), BF16) + _nbytes((d, LANES), F32)
                                 + _nbytes((tm, tn), BF16) + _nbytes((tm, LANES), F32),
                                 _nbytes((tm, d), BF16)),
        name="inproj",
    )(x2, gain, scale, shift, w_main, w_small)


def _foxpre_kernel(s_ref, fb_ref, kx_ref, cs_ref, carry_ref, *, tb):
    i = pl.program_id(0)

    @pl.when(i == 0)
    def _():
        carry_ref[...] = jnp.zeros_like(carry_ref)

    z = s_ref[...] + fb_ref[...]
    log_f = jnp.minimum(z, 0.0) - jnp.log1p(jnp.exp(-jnp.abs(z)))
    r = lax.broadcasted_iota(jnp.int32, (tb, tb), 0)
    c = lax.broadcasted_iota(jnp.int32, (tb, tb), 1)
    tri = jnp.where(c <= r, 1.0, 0.0).astype(BF16)
    hi, mid, lo = _split3(log_f)
    cum3 = _dot(tri, jnp.concatenate([hi, mid, lo], axis=1))
    cum = cum3[:, :LANES] + cum3[:, LANES:2 * LANES] + cum3[:, 2 * LANES:]
    carry = carry_ref[...]
    cs_ref[...] = jnp.broadcast_to(carry * LOG2E, cs_ref.shape)
    carry_ref[...] = carry + cum[tb - 1:tb, :]

    dd = -cum * LOG2E
    lane = lax.broadcasted_iota(jnp.int32, (tb, HEAD_DIM), 1)
    for h in range(HEADS):
        col = jnp.broadcast_to(dd[:, F_LANE + h:F_LANE + h + 1], (tb, HEAD_DIM))
        c_hi = col.astype(BF16).astype(F32)
        c_mid = (col - c_hi).astype(BF16).astype(F32)
        c_lo = col - c_hi - c_mid
        kx = jnp.where(lane == 0, c_hi, jnp.where(lane == 1, c_mid, jnp.where(lane == 2, c_lo, 0.0)))
        kx_ref[:, h * HEAD_DIM:(h + 1) * HEAD_DIM] = kx.astype(BF16)


def _fox_prologue(small, fbias_row, tb):
    t = small.shape[0]
    nblk = t // tb
    return pl.pallas_call(
        functools.partial(_foxpre_kernel, tb=tb),
        grid=(nblk,),
        in_specs=[pl.BlockSpec((tb, LANES), lambda i: (i, 0)),
                  pl.BlockSpec((1, LANES), lambda i: (0, 0))],
        out_specs=[pl.BlockSpec((tb, GROUP_W), lambda i: (i, 0)),
                   pl.BlockSpec((SUBLANES, LANES), lambda i: (i, 0))],
        out_shape=[jax.ShapeDtypeStruct((t, GROUP_W), BF16),
                   jax.ShapeDtypeStruct((nblk * SUBLANES, LANES), F32)],
        scratch_shapes=[pltpu.VMEM((1, LANES), F32)],
        compiler_params=_cparams(("arbitrary",), _nbytes((tb, LANES), F32) + _nbytes((tb, GROUP_W), BF16)),
        name="foxpre",
    )(small, fbias_row)


V_ROWS = HEAD_DIM + 16


def _fox_kernel(cs_ref, q_ref, k_ref, v_ref, kx_ref, gn_ref, o_ref,
                vt_ref, qt_ref, sa_ref, sb_ref, m_ref, acc_ref, *, tk, nblk):
    tq = 2 * tk
    h = pl.program_id(0)
    i = pl.program_id(1)

    @pl.when(i == 0)
    def _():
        row = lax.broadcasted_iota(jnp.int32, (V_ROWS - HEAD_DIM, tk), 0)
        tail = jnp.where(row == 0, 1.0, 0.0).astype(BF16)

        def fill(j, carry):
            off = pl.multiple_of(j * tk, tk)
            vt = v_ref[pl.ds(off, tk), :].astype(F32).T.astype(BF16)
            vt_ref[j] = jnp.concatenate([vt, tail], axis=0)
            return carry

        lax.fori_loop(0, nblk, fill, 0)

    row = lax.broadcasted_iota(jnp.int32, (HEAD_DIM, tq), 0)
    qt_ref[0:HEAD_DIM, :] = q_ref[...].astype(F32).T.astype(BF16)
    qt_ref[HEAD_DIM:2 * HEAD_DIM, :] = jnp.where(row < 3, 1.0, 0.0).astype(BF16)
    m_ref[...] = jnp.full_like(m_ref, NEG_INF)
    acc_ref[...] = jnp.zeros_like(acc_ref)
    first = 2 * i
    cq = cs_ref[h * nblk + first]

    def scores(j, s_ref):
        off = pl.multiple_of(j * tk, tk)
        k_aug = jnp.concatenate([k_ref[pl.ds(off, tk), :], kx_ref[pl.ds(off, tk), :]], axis=1)
        s_ref[...] = _dot(k_aug, qt_ref[...])

    def softmax_pv(j, s_ref, key0=None):
        s = s_ref[...]
        if key0 is not None:
            key = lax.broadcasted_iota(jnp.int32, (tk, tq), 0) + key0
            qry = lax.broadcasted_iota(jnp.int32, (tk, tq), 1)
            s = jnp.where(key <= qry, s, NEG_INF)
        delta = cq - cs_ref[h * nblk + j]
        m_old = m_ref[...]
        m_new = jnp.maximum(m_old, jnp.max(s, axis=0, keepdims=True) + delta)
        p = jnp.exp2(s - (m_new - delta))
        acc_ref[...] = acc_ref[...] * jnp.exp2(m_old - m_new) + _dot(vt_ref[j], p.astype(BF16))
        m_ref[...] = m_new

    scores(0, sa_ref)

    def pair_step(j0):
        scores(j0 + 1, sb_ref)
        softmax_pv(j0, sa_ref)
        scores(j0 + 2, sa_ref)
        softmax_pv(j0 + 1, sb_ref)

    def quad_body(jj, carry):
        pair_step(4 * jj)
        pair_step(4 * jj + 2)
        return carry

    lax.fori_loop(0, lax.shift_right_logical(i, 1), quad_body, 0)

    @pl.when(lax.bitwise_and(i, 1) == 1)
    def _():
        pair_step(first - 2)

    scores(first + 1, sb_ref)
    softmax_pv(first, sa_ref, key0=0)
    softmax_pv(first + 1, sb_ref, key0=tk)

    acc = acc_ref[...]
    o = acc[:HEAD_DIM, :] / acc[HEAD_DIM:HEAD_DIM + 1, :]
    o = o * lax.rsqrt(jnp.mean(o * o, axis=0, keepdims=True) + EPS) * gn_ref[...]
    o_ref[...] = o.T.astype(BF16)


def _fox_attention(cs_flat, main, kx, gnorm_col, tk):
    t = main.shape[0]
    tq = 2 * tk
    nblk = t // tk
    grid_spec = pltpu.PrefetchScalarGridSpec(
        num_scalar_prefetch=1,
        grid=(HEADS, t // tq),
        in_specs=[pl.BlockSpec((tq, HEAD_DIM), lambda h, i, cs: (i, h)),
                  pl.BlockSpec((t, HEAD_DIM), lambda h, i, cs: (0, HEADS + h)),
                  pl.BlockSpec((t, HEAD_DIM), lambda h, i, cs: (0, 2 * HEADS + h)),
                  pl.BlockSpec((t, HEAD_DIM), lambda h, i, cs: (0, h)),
                  pl.BlockSpec((HEAD_DIM, 1), lambda h, i, cs: (0, 0))],
        out_specs=pl.BlockSpec((tq, HEAD_DIM), lambda h, i, cs: (i, h)),
        scratch_shapes=[pltpu.VMEM((nblk, V_ROWS, tk), BF16), pltpu.VMEM((2 * HEAD_DIM, tq), BF16),
                        pltpu.VMEM((tk, tq), F32), pltpu.VMEM((tk, tq), F32),
                        pltpu.VMEM((1, tq), F32), pltpu.VMEM((V_ROWS, tq), F32)],
    )
    return pl.pallas_call(
        functools.partial(_fox_kernel, tk=tk, nblk=nblk),
        grid_spec=grid_spec,
        out_shape=jax.ShapeDtypeStruct((t, GROUP_W), BF16),
        compiler_params=_cparams(("arbitrary", "arbitrary"),
                                 3 * _nbytes((t, HEAD_DIM), BF16) + 2 * _nbytes((tq, HEAD_DIM), BF16),
                                 _nbytes((nblk, V_ROWS, tk), BF16) + _nbytes((2 * HEAD_DIM, tq), BF16)
                                 + 2 * _nbytes((tk, tq), F32) + _nbytes((V_ROWS + SUBLANES, tq), F32)),
        name="fox",
    )(cs_flat, main, main, main, kx, gnorm_col)


def _gdn_kernel(q_ref, k_ref, v_ref, gate_ref, s_ref, wq_ref, wk_ref, wv_ref, prm_ref, gn_ref, o_ref,
                xq_ref, xk_ref, xv_ref, yq_ref, yk_ref, yv_ref, state_ref, *, blk):
    i = pl.program_id(0)
    halo = SUBLANES

    @pl.when(i == 0)
    def _():
        state_ref[...] = jnp.zeros_like(state_ref)
        for x_ref in (xq_ref, xk_ref, xv_ref):
            x_ref[0:halo, :] = jnp.zeros((halo, GROUP_W), F32)

    def conv_silu(src_ref, x_ref, w_ref):
        x_ref[halo:halo + blk, :] = src_ref[...].astype(F32)
        y = w_ref[GDN_CONV - 1:GDN_CONV, :] * x_ref[halo:halo + blk, :]
        for tap in range(GDN_CONV - 1):
            shift = GDN_CONV - 1 - tap
            y = y + w_ref[tap:tap + 1, :] * x_ref[halo - shift:halo - shift + blk, :]
        x_ref[0:halo, :] = x_ref[blk:blk + halo, :]
        return _silu(y)

    def l2norm_heads(y, dst_ref, mult):
        for h in range(HEADS):
            yh = y[:, h * HEAD_DIM:(h + 1) * HEAD_DIM]
            inv = lax.rsqrt(jnp.sum(yh * yh, axis=-1, keepdims=True) + EPS) * mult
            dst_ref[:, h * HEAD_DIM:(h + 1) * HEAD_DIM] = yh * inv

    l2norm_heads(conv_silu(q_ref, xq_ref, wq_ref), yq_ref, HEAD_DIM ** -0.5)
    l2norm_heads(conv_silu(k_ref, xk_ref, wk_ref), yk_ref, 1.0)
    yv_ref[...] = conv_silu(v_ref, xv_ref, wv_ref)

    row = lax.broadcasted_iota(jnp.int32, (CHUNK, LANES), 0)
    lane = lax.broadcasted_iota(jnp.int32, (CHUNK, LANES), 1)
    left = lane < CHUNK
    lane_in = jnp.where(left, lane, lane - CHUNK)
    tri_incl = lane_in <= row
    tri_strict = lane_in < row
    eye_pair = jnp.where(lane_in == row, 1.0, 0.0)
    m_left = jnp.where(left, 1.0, 0.0).astype(BF16)
    m_right = jnp.where(left, 0.0, 1.0).astype(BF16)
    left_row = left[0:1, :]
    r2 = lax.broadcasted_iota(jnp.int32, (2 * CHUNK, CHUNK), 0)
    c2 = lax.broadcasted_iota(jnp.int32, (2 * CHUNK, CHUNK), 1)
    tri_twice = jnp.where(c2 <= jnp.where(r2 < CHUNK, r2, r2 - CHUNK), 1.0, 0.0).astype(BF16)
    dt_bias = prm_ref[0:1, :]
    neg_a = -jnp.exp(prm_ref[1:2, :])
    gn = gn_ref[...]

    def blockdiag(x_b):
        return jnp.concatenate([x_b * m_left, x_b * m_right], axis=0)

    def split_bd(x):
        hi, lo = _split2(x)
        return hi, lo, blockdiag(hi), blockdiag(lo)

    def pair_dot3(a_hi, a_lo, b_hi_bd, b_lo_bd):
        return _dot(jnp.concatenate([a_hi, a_hi, a_lo], axis=1),
                    jnp.concatenate([b_hi_bd, b_lo_bd, b_hi_bd], axis=0))

    def blockdiag2(a, b):
        z = jnp.zeros_like(a)
        return jnp.concatenate([jnp.concatenate([a, z], axis=1), jnp.concatenate([z, b], axis=1)], axis=0)

    def group_body(gi, carry):
        nc = CHUNKS_PER_STEP
        rows = [pl.ds(pl.multiple_of((gi * nc + c) * CHUNK, CHUNK), CHUNK) for c in range(nc)]
        gc, gc2_t, beta = [], [], []
        for c in range(nc):
            slab = s_ref[rows[c], :]
            z = slab + dt_bias
            g = neg_a * (jnp.maximum(z, 0.0) + jnp.log1p(jnp.exp(-jnp.abs(z))))
            beta.append(_sigmoid(slab))
            g_hi, g_mid, g_lo = _split3(g)
            gc3 = _dot(tri_twice, jnp.concatenate([g_hi, g_mid, g_lo], axis=1))
            gc2 = gc3[:, :LANES] + gc3[:, LANES:2 * LANES] + gc3[:, 2 * LANES:]
            gc.append(gc2[:CHUNK])
            gc2_t.append(gc2.T)

        slots = [(c, h) for c in range(nc) for h in range(HEADS)]
        cols = [slice(h * HEAD_DIM, (h + 1) * HEAD_DIM) for h in range(HEADS)]
        gcol = [gc[c][:, A_LANE + h:A_LANE + h + 1] for c, h in slots]
        g_last = [gc[c][CHUNK - 1:CHUNK, A_LANE + h:A_LANE + h + 1] for c, h in slots]
        e_gc = [jnp.exp(x) for x in gcol]
        e_tail = [jnp.exp(gl - x) for gl, x in zip(g_last, gcol)]
        bcol = [beta[c][:, B_LANE + h:B_LANE + h + 1] for c, h in slots]
        q = [yq_ref[rows[c], cols[h]] for c, h in slots]
        k = [yk_ref[rows[c], cols[h]] for c, h in slots]
        kb = [x * b for x, b in zip(k, bcol)]
        vb = [yv_ref[rows[c], cols[h]] * bcol[c * HEADS + h] for c, h in slots]
        k_b = [x.astype(BF16) for x in k]
        pairs = [(c, c * HEADS + 2 * p, c * HEADS + 2 * p + 1) for c in range(nc) for p in range(HEADS // 2)]

        decay, low, attn = [], [], []
        for c, a, b in pairs:
            ha, hb = a - c * HEADS, b - c * HEADS
            diff = (jnp.where(left, gcol[a], gcol[b])
                    - jnp.where(left_row, gc2_t[c][A_LANE + ha:A_LANE + ha + 1, :],
                                gc2_t[c][A_LANE + hb:A_LANE + hb + 1, :]))
            decay.append(jnp.exp(jnp.where(tri_incl, diff, NEG_INF)))
        for p, (c, a, b) in enumerate(pairs):
            lhs = jnp.concatenate([jnp.concatenate([kb[a], kb[b]], axis=1),
                                   jnp.concatenate([q[a], q[b]], axis=1)], axis=0).astype(BF16)
            kk_qk = _dot_nt(lhs, blockdiag2(k_b[a], k_b[b]))
            low.append(jnp.where(tri_strict, kk_qk[:CHUNK] * decay[p], 0.0))
            attn.append((kk_qk[CHUNK:] * decay[p]).astype(BF16))

        power = [pair_dot3(*split_bd(-l)) for l in low]
        inv = [eye_pair - l for l in low]
        for _ in range(int(math.log2(CHUNK)) - 2):
            both = []
            for s, pw in zip(inv, power):
                hi, lo = _split2(jnp.concatenate([s, pw], axis=0))
                both.append(pair_dot3(hi, lo, blockdiag(hi[CHUNK:]), blockdiag(lo[CHUNK:])))
            inv = [s + r[:CHUNK] for s, r in zip(inv, both)]
            power = [r[CHUNK:] for r in both]
        inv = [s + pair_dot3(*_split2(s), *split_bd(pw)[2:]) for s, pw in zip(inv, power)]

        u, w = [None] * len(slots), [None] * len(slots)
        for p, (c, a, b) in enumerate(pairs):
            xa_hi, xa_lo = _split2(jnp.concatenate([vb[a], kb[a] * e_gc[a]], axis=1))
            xb_hi, xb_lo = _split2(jnp.concatenate([vb[b], kb[b] * e_gc[b]], axis=1))
            x_hi = blockdiag2(xa_hi, xb_hi)
            t_hi, t_lo = _split2(inv[p])
            sol = _dot(jnp.concatenate([t_hi, t_hi, t_lo], axis=1),
                       jnp.concatenate([x_hi, blockdiag2(xa_lo, xb_lo), x_hi], axis=0))
            u[a], w[a], u[b], w[b] = (sol[:, n * HEAD_DIM:(n + 1) * HEAD_DIM] for n in range(4))
        wq = [jnp.concatenate([w[s], q[s] * e_gc[s]], axis=0).astype(BF16) for s in range(len(slots))]
        k_tail = [(k[s] * e_tail[s]).astype(BF16) for s in range(len(slots))]
        e_last = [jnp.exp(x) for x in g_last]

        for c in range(nc):
            base = c * HEADS
            state = [state_ref[h] for h in range(HEADS)]
            ws_qs = [_dot(wq[base + h], state[h].astype(BF16)) for h in range(HEADS)]
            v_new = [(u[base + h] - ws_qs[h][:CHUNK]).astype(BF16) for h in range(HEADS)]
            for h in range(HEADS):
                state_ref[h] = state[h] * e_last[base + h] + _dot_tn(k_tail[base + h], v_new[h])
            for p in range(HEADS // 2):
                a, b = 2 * p, 2 * p + 1
                o_pair = _dot(attn[c * (HEADS // 2) + p], blockdiag2(v_new[a], v_new[b]))
                for h, o_mm in ((a, o_pair[:, :HEAD_DIM]), (b, o_pair[:, HEAD_DIM:])):
                    o = ws_qs[h][CHUNK:] + o_mm
                    gate = gate_ref[rows[c], cols[h]].astype(F32)
                    on = o * lax.rsqrt(jnp.mean(o * o, axis=-1, keepdims=True) + EPS) * gn * _silu(gate)
                    o_ref[rows[c], cols[h]] = on.astype(BF16)
        return carry

    lax.fori_loop(0, blk // (CHUNK * CHUNKS_PER_STEP), group_body, 0)


def _gated_deltanet(main, small, conv_w, prm, gnorm, blk=512):
    t = main.shape[0]
    qcol = 3 * GROUP_W // GROUP_W
    big = lambda cb: pl.BlockSpec((blk, GROUP_W), lambda i: (i, cb))
    wspec = lambda cb: pl.BlockSpec((GDN_CONV, GROUP_W), lambda i: (0, cb))
    small_vec = lambda rows: pl.BlockSpec((rows, LANES), lambda i: (0, 0))
    return pl.pallas_call(
        functools.partial(_gdn_kernel, blk=blk),
        grid=(t // blk,),
        in_specs=[big(qcol), big(qcol + 1), big(qcol + 2), big(qcol + 3),
                  pl.BlockSpec((blk, LANES), lambda i: (i, 0)),
                  wspec(0), wspec(1), wspec(2),
                  small_vec(SUBLANES), small_vec(1)],
        out_specs=pl.BlockSpec((blk, GROUP_W), lambda i: (i, 0)),
        out_shape=jax.ShapeDtypeStruct((t, GROUP_W), BF16),
        scratch_shapes=[pltpu.VMEM((blk + SUBLANES, GROUP_W), F32)] * 3
                       + [pltpu.VMEM((blk, GROUP_W), F32)] * 3
                       + [pltpu.VMEM((HEADS, HEAD_DIM, HEAD_DIM), F32)],
        compiler_params=_cparams(("arbitrary",),
                                 5 * _nbytes((blk, GROUP_W), BF16) + _nbytes((blk, LANES), F32),
                                 6 * _nbytes((blk + SUBLANES, GROUP_W), F32)),
        name="gdn",
    )(main, main, main, main, small, conv_w, conv_w, conv_w, prm, gnorm)


def _outproj_kernel(oa_ref, ob_ref, w_ref, x_ref, g_ref, ga_ref, o_ref):
    y = _dot(oa_ref[...], w_ref[0:GROUP_W, :]) + _dot(ob_ref[...], w_ref[GROUP_W:2 * GROUP_W, :])
    yn = y * lax.rsqrt(jnp.mean(y * y, axis=-1, keepdims=True) + EPS) * g_ref[...]
    o_ref[...] = x_ref[...] + ga_ref[...] * yn


def _out_projection(o_a, o_b, w_out, x2, gain, gate, tm=512):
    t, d = x2.shape
    row = lambda i: (i, 0)
    vec = lambda i: (0, 0)
    return pl.pallas_call(
        _outproj_kernel,
        grid=(t // tm,),
        in_specs=[pl.BlockSpec((tm, GROUP_W), row), pl.BlockSpec((tm, GROUP_W), row),
                  pl.BlockSpec((2 * GROUP_W, d), vec),
                  pl.BlockSpec((tm, d), row),
                  pl.BlockSpec((1, d), vec), pl.BlockSpec((1, d), vec)],
        out_specs=pl.BlockSpec((tm, d), row),
        out_shape=jax.ShapeDtypeStruct((t, d), F32),
        compiler_params=_cparams(("arbitrary",),
                                 2 * _nbytes((tm, GROUP_W), BF16) + _nbytes((2 * GROUP_W, d), BF16)
                                 + 2 * _nbytes((tm, d), F32)),
        name="outproj",
    )(o_a, o_b, w_out, x2, gain, gate)


def _ffn_kernel(x_ref, g_ref, sc_ref, sh_ref, wg_ref, wv_ref, cwg_ref, cwv_ref, cbg_ref, cbv_ref, wd_ref,
                pg_ref, ga_ref, o_ref, h_ref, acc_ref, ug0_ref, uv0_ref, ug1_ref, uv1_ref, cg_ref, cv_ref,
                *, tm, tf, nf):
    i = pl.program_id(0)
    j = pl.program_id(1)
    halo = SUBLANES
    bufs = ((ug0_ref, uv0_ref), (ug1_ref, uv1_ref))
    n_pieces = 2 * (tf // MXU_COLS)
    rows_per = tm // n_pieces

    def up_piece(slot, p):
        part, c0 = divmod(p * MXU_COLS, tf)
        w_ref = (wg_ref, wv_ref)[part]
        bufs[slot][part][halo:halo + tm, c0:c0 + MXU_COLS] = _dot(h_ref[...], w_ref[:, c0:c0 + MXU_COLS])

    def conv(u_ref, cw_ref, cb_ref, r0):
        y = cb_ref[...]
        for tap in range(FFN_CONV):
            shift = FFN_CONV - 1 - tap
            y = y + cw_ref[tap:tap + 1, :] * u_ref[halo + r0 - shift:halo + r0 - shift + rows_per, :]
        return y

    def glu_down_piece(slot, p):
        r0 = p * rows_per
        gate = conv(bufs[slot][0], cwg_ref, cbg_ref, r0)
        val = conv(bufs[slot][1], cwv_ref, cbv_ref, r0)
        act = (_silu(gate) * val).astype(BF16)
        acc_ref[r0:r0 + rows_per, :] += _dot(act, wd_ref[...])

    def step(slot_up, slot_glu):
        if slot_glu is not None:
            for u_ref, carry_ref in zip(bufs[slot_glu], (cg_ref, cv_ref)):
                u_ref[0:halo, :] = jnp.where(i == 0, 0.0, carry_ref[j - 1])
        for p in range(n_pieces):
            if slot_up is not None:
                up_piece(slot_up, p)
            if slot_glu is not None:
                glu_down_piece(slot_glu, p)
        if slot_glu is not None:
            for u_ref, carry_ref in zip(bufs[slot_glu], (cg_ref, cv_ref)):
                carry_ref[j - 1] = u_ref[tm:tm + halo, :]

    @pl.when(j == 0)
    def _():
        h_ref[...] = _norm_modulate(x_ref[...], g_ref[...], sc_ref[...], sh_ref[...]).astype(BF16)
        acc_ref[...] = jnp.zeros_like(acc_ref)
        step(0, None)

    middle = jnp.logical_and(j > 0, j < nf)
    odd = lax.bitwise_and(j, 1) == 1

    @pl.when(jnp.logical_and(middle, odd))
    def _():
        step(1, 0)

    @pl.when(jnp.logical_and(middle, jnp.logical_not(odd)))
    def _():
        step(0, 1)

    @pl.when(j == nf)
    def _():
        step(None, (nf - 1) % 2)
        y = acc_ref[...]
        yn = y * lax.rsqrt(jnp.mean(y * y, axis=-1, keepdims=True) + EPS) * pg_ref[...]
        o_ref[...] = x_ref[...] + ga_ref[...] * yn


def _conv_ffn(x2, gain, scale, shift, w_up, conv_w, conv_b, w_down, post_gain, gate, tm=512, tf=512):
    t, d = x2.shape
    d_ff = w_down.shape[0]
    nf = d_ff // tf
    row = lambda i, j: (i, 0)
    vec = lambda i, j: (0, 0)
    gcol = lambda i, j: (0, jnp.minimum(j, nf - 1))
    vcol = lambda i, j: (0, nf + jnp.minimum(j, nf - 1))
    gprev = lambda i, j: (0, jnp.maximum(j - 1, 0))
    vprev = lambda i, j: (0, nf + jnp.maximum(j - 1, 0))
    return pl.pallas_call(
        functools.partial(_ffn_kernel, tm=tm, tf=tf, nf=nf),
        grid=(t // tm, nf + 1),
        in_specs=[pl.BlockSpec((tm, d), row),
                  pl.BlockSpec((1, d), vec), pl.BlockSpec((1, d), vec), pl.BlockSpec((1, d), vec),
                  pl.BlockSpec((d, tf), gcol), pl.BlockSpec((d, tf), vcol),
                  pl.BlockSpec((FFN_CONV, tf), gprev), pl.BlockSpec((FFN_CONV, tf), vprev),
                  pl.BlockSpec((1, tf), gprev), pl.BlockSpec((1, tf), vprev),
                  pl.BlockSpec((tf, d), lambda i, j: (jnp.maximum(j - 1, 0), 0)),
                  pl.BlockSpec((1, d), vec), pl.BlockSpec((1, d), vec)],
        out_specs=pl.BlockSpec((tm, d), row),
        out_shape=jax.ShapeDtypeStruct((t, d), F32),
        scratch_shapes=[pltpu.VMEM((tm, d), BF16), pltpu.VMEM((tm, d), F32)]
                       + [pltpu.VMEM((tm + SUBLANES, tf), F32)] * 4
                       + [pltpu.VMEM((nf, SUBLANES, tf), F32)] * 2,
        compiler_params=_cparams(("arbitrary", "arbitrary"),
                                 2 * _nbytes((tm, d), F32) + 3 * _nbytes((d, tf), BF16),
                                 _nbytes((tm, d), BF16) + _nbytes((tm, d), F32)
                                 + 4 * _nbytes((tm + SUBLANES, tf), F32) + 2 * _nbytes((nf, SUBLANES, tf), F32)),
        name="ffn",
    )(x2, gain, scale, shift, w_up, w_up, conv_w, conv_w, conv_b, conv_b, w_down, post_gain, gate)


def _lane_row(vals, lane0):
    return jnp.zeros((1, LANES), F32).at[0, lane0:lane0 + HEADS].set(vals.astype(F32))


def _layer(x2, cond, p, attn_block):
    d = x2.shape[1]
    mod = _ada_mod(cond, p["w_ada"], p["b_ada"])
    sh_m, sc_m, ga_m, sh_f, sc_f, ga_f = (mod[:, k * d:(k + 1) * d] for k in range(6))

    w_in = p["w_in"]
    o_f = 3 * GROUP_W
    o_g = o_f + HEADS
    o_a = o_g + 3 * GROUP_W
    o_gate = o_a + 2 * HEADS
    w_main = jnp.concatenate([w_in[:, :o_f].astype(BF16), w_in[:, o_g:o_a].astype(BF16),
                              w_in[:, o_gate:].astype(BF16)], axis=1)
    w_small = jnp.concatenate([w_in[:, o_f:o_g], w_in[:, o_a:o_gate],
                               jnp.zeros((d, LANES - 3 * HEADS), F32)], axis=1)

    main, small = _in_projection(x2, p["norm_mix_pre"].reshape(1, d), sc_m, sh_m, w_main, w_small)

    kx, cs = _fox_prologue(small, _lane_row(p["fox_forget_bias"], F_LANE), attn_block)
    nblk = x2.shape[0] // attn_block
    cs_flat = cs.reshape(nblk, SUBLANES, LANES)[:, 0, F_LANE:F_LANE + HEADS].T.reshape(-1)
    o_fox = _fox_attention(cs_flat, main, kx, p["fox_out_norm"].reshape(HEAD_DIM, 1), attn_block)

    prm = jnp.concatenate([_lane_row(p["gdn_dt_bias"], A_LANE), _lane_row(p["gdn_A_log"], A_LANE),
                           jnp.zeros((SUBLANES - 2, LANES), F32)], axis=0)
    o_gdn = _gated_deltanet(main, small, p["gdn_conv_w"], prm, p["gdn_out_norm"].reshape(1, HEAD_DIM))

    x2 = _out_projection(o_fox, o_gdn, p["w_out"].astype(BF16), x2, p["norm_mix_post"].reshape(1, d), ga_m)

    return _conv_ffn(x2, p["norm_ffn_pre"].reshape(1, d), sc_f, sh_f, p["w_up"].astype(BF16),
                     p["ffn_conv_w"], p["ffn_conv_b"].reshape(1, -1), p["w_down"].astype(BF16),
                     p["norm_ffn_post"].reshape(1, d), ga_f)


def kernel(x, c, w_ada, b_ada, norm_mix_pre, norm_mix_post, w_in, fox_forget_bias, fox_out_norm, gdn_conv_w, gdn_A_log, gdn_dt_bias, gdn_out_norm, w_out, norm_ffn_pre, norm_ffn_post, w_up, ffn_conv_w, ffn_conv_b, w_down):
    b, t, d = x.shape
    assert b == 1, "single-sequence kernel"
    params = dict(w_ada=w_ada, b_ada=b_ada, norm_mix_pre=norm_mix_pre, norm_mix_post=norm_mix_post, w_in=w_in,
                  fox_forget_bias=fox_forget_bias, fox_out_norm=fox_out_norm, gdn_conv_w=gdn_conv_w,
                  gdn_A_log=gdn_A_log, gdn_dt_bias=gdn_dt_bias, gdn_out_norm=gdn_out_norm, w_out=w_out,
                  norm_ffn_pre=norm_ffn_pre, norm_ffn_post=norm_ffn_post, w_up=w_up, ffn_conv_w=ffn_conv_w,
                  ffn_conv_b=ffn_conv_b, w_down=w_down)
    attn_block = min(512, t)
    x2 = x.reshape(t, d)
    for l in range(w_ada.shape[0]):
        x2 = _layer(x2, c, {k: v[l] for k, v in params.items()}, attn_block)
    return x2.reshape(b, t, d)
```

```python
import functools
import math

import jax
import jax.numpy as jnp
from jax import lax
from jax.experimental import pallas as pl
from jax.experimental.pallas import tpu as pltpu

F32 = jnp.float32
BF16 = jnp.bfloat16

EPS = 1e-6
HEADS = 8
HEAD_DIM = 128
CHUNK = 64
CHUNKS_PER_STEP = 4
GDN_CONV = 4
FFN_CONV = 3
LANES = 128
SUBLANES = 8
GROUP_W = HEADS * HEAD_DIM
LOG2E = math.log2(math.e)
NEG_INF = float("-inf")

F_LANE, A_LANE, B_LANE = 0, HEADS, 2 * HEADS


MIB = 1024 * 1024
V7X_VMEM_BYTES = 64 * MIB
TEMP_ALLOWANCE_BYTES = 22 * MIB


def _nbytes(shape, dtype):
    return math.prod(shape) * jnp.dtype(dtype).itemsize


def _cparams(semantics, window_bytes, scratch_bytes=0, flags=None):
    want = 2 * window_bytes + scratch_bytes + TEMP_ALLOWANCE_BYTES
    return pltpu.CompilerParams(dimension_semantics=semantics, flags=flags,
                                vmem_limit_bytes=min(want, V7X_VMEM_BYTES - 4 * MIB))


def _sigmoid(x):
    return 0.5 * jnp.tanh(0.5 * x) + 0.5


def _silu(x):
    h = 0.5 * x
    return h + h * jnp.tanh(h)


def _split2(a):
    hi = a.astype(BF16)
    lo = (a - hi.astype(F32)).astype(BF16)
    return hi, lo


def _split3(a):
    hi = a.astype(BF16)
    r = a - hi.astype(F32)
    mid = r.astype(BF16)
    lo = (r - mid.astype(F32)).astype(BF16)
    return hi, mid, lo


def _dot(a, b):
    return jnp.dot(a, b, preferred_element_type=F32)


def _dot_nt(a, b):
    return lax.dot_general(a, b, (((1,), (1,)), ((), ())), preferred_element_type=F32)


def _dot_tn(a, b):
    return lax.dot_general(a, b, (((0,), (0,)), ((), ())), preferred_element_type=F32)


def _ada_kernel(c_ref, w_ref, b_ref, o_ref):
    cond = _silu(c_ref[...])
    o_ref[...] = jnp.sum(cond * w_ref[...], axis=0, keepdims=True) + b_ref[...]


def _ada_mod(c, w_ada, b_ada, tn=1024):
    d, n = w_ada.shape
    return pl.pallas_call(
        _ada_kernel,
        grid=(n // tn,),
        in_specs=[pl.BlockSpec((d, 1), lambda j: (0, 0)),
                  pl.BlockSpec((d, tn), lambda j: (0, j)),
                  pl.BlockSpec((1, tn), lambda j: (0, j))],
        out_specs=pl.BlockSpec((1, tn), lambda j: (0, j)),
        out_shape=jax.ShapeDtypeStruct((1, n), F32),
        compiler_params=_cparams(("arbitrary",), _nbytes((d, tn), F32)),
        name="ada",
    )(c.reshape(d, 1), w_ada, b_ada.reshape(1, n))


def _norm_modulate(x, gain, scale, shift):
    y = x * lax.rsqrt(jnp.mean(x * x, axis=-1, keepdims=True) + EPS) * gain
    return y * (1.0 + scale) + shift


def _inproj_kernel(x_ref, g_ref, sc_ref, sh_ref, wm_ref, ws_ref, om_ref, os_ref, h_ref, *, q_scale):
    j = pl.program_id(1)

    @pl.when(j == 0)
    def _():
        h = _norm_modulate(x_ref[...], g_ref[...], sc_ref[...], sh_ref[...])
        h_b = h.astype(BF16)
        h_ref[...] = h_b
        w_hi, w_lo = _split2(ws_ref[...])
        both = _dot(h_b, jnp.concatenate([w_hi, w_lo], axis=1))
        os_ref[...] = both[:, :LANES] + both[:, LANES:]

    acc = _dot(h_ref[...], wm_ref[...])
    om_ref[...] = (acc * jnp.where(j == 0, q_scale, 1.0)).astype(BF16)


def _in_projection(x2, gain, scale, shift, w_main, w_small, tm=1024, tn=1024):
    t, d = x2.shape
    n = w_main.shape[1]
    row = lambda i, j: (i, 0)
    vec = lambda i, j: (0, 0)
    q_scale = HEAD_DIM ** -0.5 * LOG2E
    return pl.pallas_call(
        functools.partial(_inproj_kernel, q_scale=q_scale),
        grid=(t // tm, n // tn),
        in_specs=[pl.BlockSpec((tm, d), row),
                  pl.BlockSpec((1, d), vec), pl.BlockSpec((1, d), vec), pl.BlockSpec((1, d), vec),
                  pl.BlockSpec((d, tn), lambda i, j: (0, j)),
                  pl.BlockSpec((d, LANES), vec)],
        out_specs=[pl.BlockSpec((tm, tn), lambda i, j: (i, j)),
                   pl.BlockSpec((tm, LANES), row)],
        out_shape=[jax.ShapeDtypeStruct((t, n), BF16), jax.ShapeDtypeStruct((t, LANES), F32)],
        scratch_shapes=[pltpu.VMEM((tm, d), BF16)],
        compiler_params=_cparams(("arbitrary", "arbitrary"),
                                 _nbytes((tm, d), F32) + _nbytes((d, tn), BF16) + _nbytes((d, LANES), F32)
                                 + _nbytes((tm, tn), BF16) + _nbytes((tm, LANES), F32),
                                 _nbytes((tm, d), BF16)),
        name="inproj",
    )(x2, gain, scale, shift, w_main, w_small)


def _foxpre_kernel(s_ref, fb_ref, kx_ref, cs_ref, carry_ref, *, tb):
    i = pl.program_id(0)

    @pl.when(i == 0)
    def _():
        carry_ref[...] = jnp.zeros_like(carry_ref)

    z = s_ref[...] + fb_ref[...]
    log_f = jnp.minimum(z, 0.0) - jnp.log1p(jnp.exp(-jnp.abs(z)))
    r = lax.broadcasted_iota(jnp.int32, (tb, tb), 0)
    c = lax.broadcasted_iota(jnp.int32, (tb, tb), 1)
    tri = jnp.where(c <= r, 1.0, 0.0).astype(BF16)
    hi, mid, lo = _split3(log_f)
    cum3 = _dot(tri, jnp.concatenate([hi, mid, lo], axis=1))
    cum = cum3[:, :LANES] + cum3[:, LANES:2 * LANES] + cum3[:, 2 * LANES:]
    carry = carry_ref[...]
    cs_ref[...] = jnp.broadcast_to(carry * LOG2E, cs_ref.shape)
    carry_ref[...] = carry + cum[tb - 1:tb, :]

    dd = -cum * LOG2E
    lane = lax.broadcasted_iota(jnp.int32, (tb, HEAD_DIM), 1)
    for h in range(HEADS):
        col = jnp.broadcast_to(dd[:, F_LANE + h:F_LANE + h + 1], (tb, HEAD_DIM))
        c_hi = col.astype(BF16).astype(F32)
        c_mid = (col - c_hi).astype(BF16).astype(F32)
        c_lo = col - c_hi - c_mid
        kx = jnp.where(lane == 0, c_hi, jnp.where(lane == 1, c_mid, jnp.where(lane == 2, c_lo, 0.0)))
        kx_ref[:, h * HEAD_DIM:(h + 1) * HEAD_DIM] = kx.astype(BF16)


def _fox_prologue(small, fbias_row, tb):
    t = small.shape[0]
    nblk = t // tb
    return pl.pallas_call(
        functools.partial(_foxpre_kernel, tb=tb),
        grid=(nblk,),
        in_specs=[pl.BlockSpec((tb, LANES), lambda i: (i, 0)),
                  pl.BlockSpec((1, LANES), lambda i: (0, 0))],
        out_specs=[pl.BlockSpec((tb, GROUP_W), lambda i: (i, 0)),
                   pl.BlockSpec((SUBLANES, LANES), lambda i: (i, 0))],
        out_shape=[jax.ShapeDtypeStruct((t, GROUP_W), BF16),
                   jax.ShapeDtypeStruct((nblk * SUBLANES, LANES), F32)],
        scratch_shapes=[pltpu.VMEM((1, LANES), F32)],
        compiler_params=_cparams(("arbitrary",), _nbytes((tb, LANES), F32) + _nbytes((tb, GROUP_W), BF16)),
        name="foxpre",
    )(small, fbias_row)


V_ROWS = HEAD_DIM + 16


def _fox_kernel(cs_ref, q_ref, k_ref, v_ref, kx_ref, gn_ref, o_ref,
                vt_ref, qt_ref, sa_ref, sb_ref, m_ref, acc_ref, *, tk, nblk):
    tq = 2 * tk
    h = pl.program_id(0)
    i = pl.program_id(1)

    @pl.when(i == 0)
    def _():
        row = lax.broadcasted_iota(jnp.int32, (V_ROWS - HEAD_DIM, tk), 0)
        tail = jnp.where(row == 0, 1.0, 0.0).astype(BF16)

        def fill(j, carry):
            off = pl.multiple_of(j * tk, tk)
            vt = v_ref[pl.ds(off, tk), :].astype(F32).T.astype(BF16)
            vt_ref[j] = jnp.concatenate([vt, tail], axis=0)
            return carry

        lax.fori_loop(0, nblk, fill, 0)

    row = lax.broadcasted_iota(jnp.int32, (HEAD_DIM, tq), 0)
    qt_ref[0:HEAD_DIM, :] = q_ref[...].astype(F32).T.astype(BF16)
    qt_ref[HEAD_DIM:2 * HEAD_DIM, :] = jnp.where(row < 3, 1.0, 0.0).astype(BF16)
    m_ref[...] = jnp.full_like(m_ref, NEG_INF)
    acc_ref[...] = jnp.zeros_like(acc_ref)
    first = 2 * i
    cq = cs_ref[h * nblk + first]

    def scores(j, s_ref):
        off = pl.multiple_of(j * tk, tk)
        k_aug = jnp.concatenate([k_ref[pl.ds(off, tk), :], kx_ref[pl.ds(off, tk), :]], axis=1)
        s_ref[...] = _dot(k_aug, qt_ref[...])

    def softmax_pv(j, s_ref, key0=None):
        s = s_ref[...]
        if key0 is not None:
            key = lax.broadcasted_iota(jnp.int32, (tk, tq), 0) + key0
            qry = lax.broadcasted_iota(jnp.int32, (tk, tq), 1)
            s = jnp.where(key <= qry, s, NEG_INF)
        delta = cq - cs_ref[h * nblk + j]
        m_old = m_ref[...]
        m_new = jnp.maximum(m_old, jnp.max(s, axis=0, keepdims=True) + delta)
        p = jnp.exp2(s - (m_new - delta))
        acc_ref[...] = acc_ref[...] * jnp.exp2(m_old - m_new) + _dot(vt_ref[j], p.astype(BF16))
        m_ref[...] = m_new

    scores(0, sa_ref)

    def pair_step(j0):
        scores(j0 + 1, sb_ref)
        softmax_pv(j0, sa_ref)
        scores(j0 + 2, sa_ref)
        softmax_pv(j0 + 1, sb_ref)

    def quad_body(jj, carry):
        pair_step(4 * jj)
        pair_step(4 * jj + 2)
        return carry

    lax.fori_loop(0, lax.shift_right_logical(i, 1), quad_body, 0)

    @pl.when(lax.bitwise_and(i, 1) == 1)
    def _():
        pair_step(first - 2)

    scores(first + 1, sb_ref)
    softmax_pv(first, sa_ref, key0=0)
    softmax_pv(first + 1, sb_ref, key0=tk)

    acc = acc_ref[...]
    o = acc[:HEAD_DIM, :] / acc[HEAD_DIM:HEAD_DIM + 1, :]
    o = o * lax.rsqrt(jnp.mean(o * o, axis=0, keepdims=True) + EPS) * gn_ref[...]
    o_ref[...] = o.T.astype(BF16)


def _fox_attention(cs_flat, main, kx, gnorm_col, tk):
    t = main.shape[0]
    tq = 2 * tk
    nblk = t // tk
    grid_spec = pltpu.PrefetchScalarGridSpec(
        num_scalar_prefetch=1,
        grid=(HEADS, t // tq),
        in_specs=[pl.BlockSpec((tq, HEAD_DIM), lambda h, i, cs: (i, h)),
                  pl.BlockSpec((t, HEAD_DIM), lambda h, i, cs: (0, HEADS + h)),
                  pl.BlockSpec((t, HEAD_DIM), lambda h, i, cs: (0, 2 * HEADS + h)),
                  pl.BlockSpec((t, HEAD_DIM), lambda h, i, cs: (0, h)),
                  pl.BlockSpec((HEAD_DIM, 1), lambda h, i, cs: (0, 0))],
        out_specs=pl.BlockSpec((tq, HEAD_DIM), lambda h, i, cs: (i, h)),
        scratch_shapes=[pltpu.VMEM((nblk, V_ROWS, tk), BF16), pltpu.VMEM((2 * HEAD_DIM, tq), BF16),
                        pltpu.VMEM((tk, tq), F32), pltpu.VMEM((tk, tq), F32),
                        pltpu.VMEM((1, tq), F32), pltpu.VMEM((V_ROWS, tq), F32)],
    )
    return pl.pallas_call(
        functools.partial(_fox_kernel, tk=tk, nblk=nblk),
        grid_spec=grid_spec,
        out_shape=jax.ShapeDtypeStruct((t, GROUP_W), BF16),
        compiler_params=_cparams(("arbitrary", "arbitrary"),
                                 3 * _nbytes((t, HEAD_DIM), BF16) + 2 * _nbytes((tq, HEAD_DIM), BF16),
                                 _nbytes((nblk, V_ROWS, tk), BF16) + _nbytes((2 * HEAD_DIM, tq), BF16)
                                 + 2 * _nbytes((tk, tq), F32) + _nbytes((V_ROWS + SUBLANES, tq), F32)),
        name="fox",
    )(cs_flat, main, main, main, kx, gnorm_col)


def _gdn_kernel(q_ref, k_ref, v_ref, gate_ref, s_ref, wq_ref, wk_ref, wv_ref, prm_ref, gn_ref, o_ref,
                xq_ref, xk_ref, xv_ref, yq_ref, yk_ref, yv_ref, state_ref, *, blk):
    i = pl.program_id(0)
    halo = SUBLANES

    @pl.when(i == 0)
    def _():
        state_ref[...] = jnp.zeros_like(state_ref)
        for x_ref in (xq_ref, xk_ref, xv_ref):
            x_ref[0:halo, :] = jnp.zeros((halo, GROUP_W), F32)

    def conv_silu(src_ref, x_ref, w_ref):
        x_ref[halo:halo + blk, :] = src_ref[...].astype(F32)
        y = w_ref[GDN_CONV - 1:GDN_CONV, :] * x_ref[halo:halo + blk, :]
        for tap in range(GDN_CONV - 1):
            shift = GDN_CONV - 1 - tap
            y = y + w_ref[tap:tap + 1, :] * x_ref[halo - shift:halo - shift + blk, :]
        x_ref[0:halo, :] = x_ref[blk:blk + halo, :]
        return _silu(y)

    def l2norm_heads(y, dst_ref, mult):
        for h in range(HEADS):
            yh = y[:, h * HEAD_DIM:(h + 1) * HEAD_DIM]
            inv = lax.rsqrt(jnp.sum(yh * yh, axis=-1, keepdims=True) + EPS) * mult
            dst_ref[:, h * HEAD_DIM:(h + 1) * HEAD_DIM] = yh * inv

    l2norm_heads(conv_silu(q_ref, xq_ref, wq_ref), yq_ref, HEAD_DIM ** -0.5)
    l2norm_heads(conv_silu(k_ref, xk_ref, wk_ref), yk_ref, 1.0)
    yv_ref[...] = conv_silu(v_ref, xv_ref, wv_ref)

    row = lax.broadcasted_iota(jnp.int32, (CHUNK, LANES), 0)
    lane = lax.broadcasted_iota(jnp.int32, (CHUNK, LANES), 1)
    left = lane < CHUNK
    lane_in = jnp.where(left, lane, lane - CHUNK)
    tri_incl = lane_in <= row
    tri_strict = lane_in < row
    eye_pair = jnp.where(lane_in == row, 1.0, 0.0)
    m_left = jnp.where(left, 1.0, 0.0).astype(BF16)
    m_right = jnp.where(left, 0.0, 1.0).astype(BF16)
    left_row = left[0:1, :]
    r2 = lax.broadcasted_iota(jnp.int32, (2 * CHUNK, CHUNK), 0)
    c2 = lax.broadcasted_iota(jnp.int32, (2 * CHUNK, CHUNK), 1)
    tri_twice = jnp.where(c2 <= jnp.where(r2 < CHUNK, r2, r2 - CHUNK), 1.0, 0.0).astype(BF16)
    dt_bias = prm_ref[0:1, :]
    neg_a = -jnp.exp(prm_ref[1:2, :])
    gn = gn_ref[...]

    def blockdiag(x_b):
        return jnp.concatenate([x_b * m_left, x_b * m_right], axis=0)

    def split_bd(x):
        hi, lo = _split2(x)
        return hi, lo, blockdiag(hi), blockdiag(lo)

    def pair_dot3(a_hi, a_lo, b_hi_bd, b_lo_bd):
        return _dot(jnp.concatenate([a_hi, a_hi, a_lo], axis=1),
                    jnp.concatenate([b_hi_bd, b_lo_bd, b_hi_bd], axis=0))

    def blockdiag2(a, b):
        z = jnp.zeros_like(a)
        return jnp.concatenate([jnp.concatenate([a, z], axis=1), jnp.concatenate([z, b], axis=1)], axis=0)

    def group_body(gi, carry):
        nc = CHUNKS_PER_STEP
        rows = [pl.ds(pl.multiple_of((gi * nc + c) * CHUNK, CHUNK), CHUNK) for c in range(nc)]
        gc, gc2_t, beta = [], [], []
        for c in range(nc):
            slab = s_ref[rows[c], :]
            z = slab + dt_bias
            g = neg_a * (jnp.maximum(z, 0.0) + jnp.log1p(jnp.exp(-jnp.abs(z))))
            beta.append(_sigmoid(slab))
            g_hi, g_mid, g_lo = _split3(g)
            gc3 = _dot(tri_twice, jnp.concatenate([g_hi, g_mid, g_lo], axis=1))
            gc2 = gc3[:, :LANES] + gc3[:, LANES:2 * LANES] + gc3[:, 2 * LANES:]
            gc.append(gc2[:CHUNK])
            gc2_t.append(gc2.T)

        slots = [(c, h) for c in range(nc) for h in range(HEADS)]
        cols = [slice(h * HEAD_DIM, (h + 1) * HEAD_DIM) for h in range(HEADS)]
        gcol = [gc[c][:, A_LANE + h:A_LANE + h + 1] for c, h in slots]
        g_last = [gc[c][CHUNK - 1:CHUNK, A_LANE + h:A_LANE + h + 1] for c, h in slots]
        e_gc = [jnp.exp(x) for x in gcol]
        e_tail = [jnp.exp(gl - x) for gl, x in zip(g_last, gcol)]
        bcol = [beta[c][:, B_LANE + h:B_LANE + h + 1] for c, h in slots]
        q = [yq_ref[rows[c], cols[h]] for c, h in slots]
        k = [yk_ref[rows[c], cols[h]] for c, h in slots]
        kb = [x * b for x, b in zip(k, bcol)]
        vb = [yv_ref[rows[c], cols[h]] * bcol[c * HEADS + h] for c, h in slots]
        k_b = [x.astype(BF16) for x in k]
        pairs = [(c, c * HEADS + 2 * p, c * HEADS + 2 * p + 1) for c in range(nc) for p in range(HEADS // 2)]

        decay, low, attn = [], [], []
        for c, a, b in pairs:
            ha, hb = a - c * HEADS, b - c * HEADS
            diff = (jnp.where(left, gcol[a], gcol[b])
                    - jnp.where(left_row, gc2_t[c][A_LANE + ha:A_LANE + ha + 1, :],
                                gc2_t[c][A_LANE + hb:A_LANE + hb + 1, :]))
            decay.append(jnp.exp(jnp.where(tri_incl, diff, NEG_INF)))
        for p, (c, a, b) in enumerate(pairs):
            lhs = jnp.concatenate([jnp.concatenate([kb[a], kb[b]], axis=1),
                                   jnp.concatenate([q[a], q[b]], axis=1)], axis=0).astype(BF16)
            kk_qk = _dot_nt(lhs, blockdiag2(k_b[a], k_b[b]))
            low.append(jnp.where(tri_strict, kk_qk[:CHUNK] * decay[p], 0.0))
            attn.append((kk_qk[CHUNK:] * decay[p]).astype(BF16))

        power = [pair_dot3(*split_bd(-l)) for l in low]
        inv = [eye_pair - l for l in low]
        for _ in range(int(math.log2(CHUNK)) - 2):
            both = []
            for s, pw in zip(inv, power):
                hi, lo = _split2(jnp.concatenate([s, pw], axis=0))
                both.append(pair_dot3(hi, lo, blockdiag(hi[CHUNK:]), blockdiag(lo[CHUNK:])))
            inv = [s + r[:CHUNK] for s, r in zip(inv, both)]
            power = [r[CHUNK:] for r in both]
        inv = [s + pair_dot3(*_split2(s), *split_bd(pw)[2:]) for s, pw in zip(inv, power)]

        u, w = [None] * len(slots), [None] * len(slots)
        for p, (c, a, b) in enumerate(pairs):
            xa_hi, xa_lo = _split2(jnp.concatenate([vb[a], kb[a] * e_gc[a]], axis=1))
            xb_hi, xb_lo = _split2(jnp.concatenate([vb[b], kb[b] * e_gc[b]], axis=1))
            x_hi = blockdiag2(xa_hi, xb_hi)
            t_hi, t_lo = _split2(inv[p])
            sol = _dot(jnp.concatenate([t_hi, t_hi, t_lo], axis=1),
                       jnp.concatenate([x_hi, blockdiag2(xa_lo, xb_lo), x_hi], axis=0))
            u[a], w[a], u[b], w[b] = (sol[:, n * HEAD_DIM:(n + 1) * HEAD_DIM] for n in range(4))
        wq = [jnp.concatenate([w[s], q[s] * e_gc[s]], axis=0).astype(BF16) for s in range(len(slots))]
        k_tail = [(k[s] * e_tail[s]).astype(BF16) for s in range(len(slots))]
        e_last = [jnp.exp(x) for x in g_last]

        for c in range(nc):
            base = c * HEADS
            state = [state_ref[h] for h in range(HEADS)]
            ws_qs = [_dot(wq[base + h], state[h].astype(BF16)) for h in range(HEADS)]
            v_new = [(u[base + h] - ws_qs[h][:CHUNK]).astype(BF16) for h in range(HEADS)]
            for h in range(HEADS):
                state_ref[h] = state[h] * e_last[base + h] + _dot_tn(k_tail[base + h], v_new[h])
            for p in range(HEADS // 2):
                a, b = 2 * p, 2 * p + 1
                o_pair = _dot(attn[c * (HEADS // 2) + p], blockdiag2(v_new[a], v_new[b]))
                for h, o_mm in ((a, o_pair[:, :HEAD_DIM]), (b, o_pair[:, HEAD_DIM:])):
                    o = ws_qs[h][CHUNK:] + o_mm
                    gate = gate_ref[rows[c], cols[h]].astype(F32)
                    on = o * lax.rsqrt(jnp.mean(o * o, axis=-1, keepdims=True) + EPS) * gn * _silu(gate)
                    o_ref[rows[c], cols[h]] = on.astype(BF16)
        return carry

    lax.fori_loop(0, blk // (CHUNK * CHUNKS_PER_STEP), group_body, 0)


def _gated_deltanet(main, small, conv_w, prm, gnorm, blk=512):
    t = main.shape[0]
    qcol = 3 * GROUP_W // GROUP_W
    big = lambda cb: pl.BlockSpec((blk, GROUP_W), lambda i: (i, cb))
    wspec = lambda cb: pl.BlockSpec((GDN_CONV, GROUP_W), lambda i: (0, cb))
    small_vec = lambda rows: pl.BlockSpec((rows, LANES), lambda i: (0, 0))
    return pl.pallas_call(
        functools.partial(_gdn_kernel, blk=blk),
        grid=(t // blk,),
        in_specs=[big(qcol), big(qcol + 1), big(qcol + 2), big(qcol + 3),
                  pl.BlockSpec((blk, LANES), lambda i: (i, 0)),
                  wspec(0), wspec(1), wspec(2),
                  small_vec(SUBLANES), small_vec(1)],
        out_specs=pl.BlockSpec((blk, GROUP_W), lambda i: (i, 0)),
        out_shape=jax.ShapeDtypeStruct((t, GROUP_W), BF16),
        scratch_shapes=[pltpu.VMEM((blk + SUBLANES, GROUP_W), F32)] * 3
                       + [pltpu.VMEM((blk, GROUP_W), F32)] * 3
                       + [pltpu.VMEM((HEADS, HEAD_DIM, HEAD_DIM), F32)],
        compiler_params=_cparams(("arbitrary",),
                                 5 * _nbytes((blk, GROUP_W), BF16) + _nbytes((blk, LANES), F32),
                                 6 * _nbytes((blk + SUBLANES, GROUP_W), F32)),
        name="gdn",
    )(main, main, main, main, small, conv_w, conv_w, conv_w, prm, gnorm)


def _outproj_kernel(oa_ref, ob_ref, w_ref, x_ref, g_ref, ga_ref, o_ref):
    y = _dot(oa_ref[...], w_ref[0:GROUP_W, :]) + _dot(ob_ref[...], w_ref[GROUP_W:2 * GROUP_W, :])
    yn = y * lax.rsqrt(jnp.mean(y * y, axis=-1, keepdims=True) + EPS) * g_ref[...]
    o_ref[...] = x_ref[...] + ga_ref[...] * yn


def _out_projection(o_a, o_b, w_out, x2, gain, gate, tm=512):
    t, d = x2.shape
    row = lambda i: (i, 0)
    vec = lambda i: (0, 0)
    return pl.pallas_call(
        _outproj_kernel,
        grid=(t // tm,),
        in_specs=[pl.BlockSpec((tm, GROUP_W), row), pl.BlockSpec((tm, GROUP_W), row),
                  pl.BlockSpec((2 * GROUP_W, d), vec),
                  pl.BlockSpec((tm, d), row),
                  pl.BlockSpec((1, d), vec), pl.BlockSpec((1, d), vec)],
        out_specs=pl.BlockSpec((tm, d), row),
        out_shape=jax.ShapeDtypeStruct((t, d), F32),
        compiler_params=_cparams(("arbitrary",),
                                 2 * _nbytes((tm, GROUP_W), BF16) + _nbytes((2 * GROUP_W, d), BF16)
                                 + 2 * _nbytes((tm, d), F32)),
        name="outproj",
    )(o_a, o_b, w_out, x2, gain, gate)


def _ffn_kernel(x_ref, g_ref, sc_ref, sh_ref, wg_ref, wv_ref, cwg_ref, cwv_ref, cbg_ref, cbv_ref, wd_ref,
                pg_ref, ga_ref, o_ref, h_ref, acc_ref, ug_ref, uv_ref, cg_ref, cv_ref, *, tm, nf):
    i = pl.program_id(0)
    j = pl.program_id(1)
    halo = SUBLANES

    @pl.when(j == 0)
    def _():
        h_ref[...] = _norm_modulate(x_ref[...], g_ref[...], sc_ref[...], sh_ref[...]).astype(BF16)
        acc_ref[...] = jnp.zeros_like(acc_ref)

    @pl.when(i == 0)
    def _():
        cg_ref[j] = jnp.zeros(cg_ref.shape[1:], F32)
        cv_ref[j] = jnp.zeros(cv_ref.shape[1:], F32)

    def up_conv(w_ref, u_ref, carry_ref, cw_ref, cb_ref):
        u = _dot(h_ref[...], w_ref[...])
        u_ref[0:halo, :] = carry_ref[j]
        u_ref[halo:halo + tm, :] = u
        carry_ref[j] = u[tm - halo:tm, :]
        y = cw_ref[FFN_CONV - 1:FFN_CONV, :] * u + cb_ref[...]
        for tap in range(FFN_CONV - 1):
            shift = FFN_CONV - 1 - tap
            y = y + cw_ref[tap:tap + 1, :] * u_ref[halo - shift:halo - shift + tm, :]
        return y

    gate = up_conv(wg_ref, ug_ref, cg_ref, cwg_ref, cbg_ref)
    val = up_conv(wv_ref, uv_ref, cv_ref, cwv_ref, cbv_ref)
    act = (gate * (1.0 / (1.0 + jnp.exp(-gate))) * val).astype(BF16)
    acc_ref[...] += _dot(act, wd_ref[...])

    @pl.when(j == nf - 1)
    def _():
        y = acc_ref[...]
        yn = y * lax.rsqrt(jnp.mean(y * y, axis=-1, keepdims=True) + EPS) * pg_ref[...]
        o_ref[...] = x_ref[...] + ga_ref[...] * yn


def _conv_ffn(x2, gain, scale, shift, w_up, conv_w, conv_b, w_down, post_gain, gate, tm=512, tf=512):
    t, d = x2.shape
    d_ff = w_down.shape[0]
    nf = d_ff // tf
    row = lambda i, j: (i, 0)
    vec = lambda i, j: (0, 0)
    gcol = lambda i, j: (0, j)
    vcol = lambda i, j: (0, nf + j)
    return pl.pallas_call(
        functools.partial(_ffn_kernel, tm=tm, nf=nf),
        grid=(t // tm, nf),
        in_specs=[pl.BlockSpec((tm, d), row),
                  pl.BlockSpec((1, d), vec), pl.BlockSpec((1, d), vec), pl.BlockSpec((1, d), vec),
                  pl.BlockSpec((d, tf), gcol), pl.BlockSpec((d, tf), vcol),
                  pl.BlockSpec((FFN_CONV, tf), gcol), pl.BlockSpec((FFN_CONV, tf), vcol),
                  pl.BlockSpec((1, tf), gcol), pl.BlockSpec((1, tf), vcol),
                  pl.BlockSpec((tf, d), lambda i, j: (j, 0)),
                  pl.BlockSpec((1, d), vec), pl.BlockSpec((1, d), vec)],
        out_specs=pl.BlockSpec((tm, d), row),
        out_shape=jax.ShapeDtypeStruct((t, d), F32),
        scratch_shapes=[pltpu.VMEM((tm, d), BF16), pltpu.VMEM((tm, d), F32),
                        pltpu.VMEM((tm + SUBLANES, tf), F32), pltpu.VMEM((tm + SUBLANES, tf), F32),
                        pltpu.VMEM((nf, SUBLANES, tf), F32), pltpu.VMEM((nf, SUBLANES, tf), F32)],
        compiler_params=_cparams(("arbitrary", "arbitrary"),
                                 2 * _nbytes((tm, d), F32) + 3 * _nbytes((d, tf), BF16),
                                 _nbytes((tm, d), BF16) + _nbytes((tm, d), F32)
                                 + 2 * _nbytes((tm + SUBLANES, tf), F32) + 2 * _nbytes((nf, SUBLANES, tf), F32)),
        name="ffn",
    )(x2, gain, scale, shift, w_up, w_up, conv_w, conv_w, conv_b, conv_b, w_down, post_gain, gate)


def _lane_row(vals, lane0):
    return jnp.zeros((1, LANES), F32).at[0, lane0:lane0 + HEADS].set(vals.astype(F32))


def _layer(x2, cond, p, attn_block):
    d = x2.shape[1]
    mod = _ada_mod(cond, p["w_ada"], p["b_ada"])
    sh_m, sc_m, ga_m, sh_f, sc_f, ga_f = (mod[:, k * d:(k + 1) * d] for k in range(6))

    w_in = p["w_in"]
    o_f = 3 * GROUP_W
    o_g = o_f + HEADS
    o_a = o_g + 3 * GROUP_W
    o_gate = o_a + 2 * HEADS
    w_main = jnp.concatenate([w_in[:, :o_f].astype(BF16), w_in[:, o_g:o_a].astype(BF16),
                              w_in[:, o_gate:].astype(BF16)], axis=1)
    w_small = jnp.concatenate([w_in[:, o_f:o_g], w_in[:, o_a:o_gate],
                               jnp.zeros((d, LANES - 3 * HEADS), F32)], axis=1)

    main, small = _in_projection(x2, p["norm_mix_pre"].reshape(1, d), sc_m, sh_m, w_main, w_small)

    kx, cs = _fox_prologue(small, _lane_row(p["fox_forget_bias"], F_LANE), attn_block)
    nblk = x2.shape[0] // attn_block
    cs_flat = cs.reshape(nblk, SUBLANES, LANES)[:, 0, F_LANE:F_LANE + HEADS].T.reshape(-1)
    o_fox = _fox_attention(cs_flat, main, kx, p["fox_out_norm"].reshape(HEAD_DIM, 1), attn_block)

    prm = jnp.concatenate([_lane_row(p["gdn_dt_bias"], A_LANE), _lane_row(p["gdn_A_log"], A_LANE),
                           jnp.zeros((SUBLANES - 2, LANES), F32)], axis=0)
    o_gdn = _gated_deltanet(main, small, p["gdn_conv_w"], prm, p["gdn_out_norm"].reshape(1, HEAD_DIM))

    x2 = _out_projection(o_fox, o_gdn, p["w_out"].astype(BF16), x2, p["norm_mix_post"].reshape(1, d), ga_m)

    return _conv_ffn(x2, p["norm_ffn_pre"].reshape(1, d), sc_f, sh_f, p["w_up"].astype(BF16),
                     p["ffn_conv_w"], p["ffn_conv_b"].reshape(1, -1), p["w_down"].astype(BF16),
                     p["norm_ffn_post"].reshape(1, d), ga_f)


def kernel(x, c, w_ada, b_ada, norm_mix_pre, norm_mix_post, w_in, fox_forget_bias, fox_out_norm, gdn_conv_w, gdn_A_log, gdn_dt_bias, gdn_out_norm, w_out, norm_ffn_pre, norm_ffn_post, w_up, ffn_conv_w, ffn_conv_b, w_down):
    b, t, d = x.shape
    assert b == 1, "single-sequence kernel"
    params = dict(w_ada=w_ada, b_ada=b_ada, norm_mix_pre=norm_mix_pre, norm_mix_post=norm_mix_post, w_in=w_in,
                  fox_forget_bias=fox_forget_bias, fox_out_norm=fox_out_norm, gdn_conv_w=gdn_conv_w,
                  gdn_A_log=gdn_A_log, gdn_dt_bias=gdn_dt_bias, gdn_out_norm=gdn_out_norm, w_out=w_out,
                  norm_ffn_pre=norm_ffn_pre, norm_ffn_post=norm_ffn_post, w_up=w_up, ffn_conv_w=ffn_conv_w,
                  ffn_conv_b=ffn_conv_b, w_down=w_down)
    attn_block = min(512, t)
    x2 = x.reshape(t, d)
    for l in range(w_ada.shape[0]):
        x2 = _layer(x2, c, {k: v[l] for k, v in params.items()}, attn_block)
    return x2.reshape(b, t, d)
```

```python
import functools
import math

import jax
import jax.numpy as jnp
from jax import lax
from jax.experimental import pallas as pl
from jax.experimental.pallas import tpu as pltpu

F32 = jnp.float32
BF16 = jnp.bfloat16

EPS = 1e-6
HEADS = 8
HEAD_DIM = 128
CHUNK = 64
CHUNKS_PER_STEP = 4
GDN_CONV = 4
FFN_CONV = 3
LANES = 128
SUBLANES = 8
GROUP_W = HEADS * HEAD_DIM
LOG2E = math.log2(math.e)
NEG_INF = float("-inf")

F_LANE, A_LANE, B_LANE = 0, HEADS, 2 * HEADS


MIB = 1024 * 1024
V7X_VMEM_BYTES = 64 * MIB
TEMP_ALLOWANCE_BYTES = 22 * MIB


def _nbytes(shape, dtype):
    return math.prod(shape) * jnp.dtype(dtype).itemsize


def _cparams(semantics, window_bytes, scratch_bytes=0, flags=None):
    want = 2 * window_bytes + scratch_bytes + TEMP_ALLOWANCE_BYTES
    return pltpu.CompilerParams(dimension_semantics=semantics, flags=flags,
                                vmem_limit_bytes=min(want, V7X_VMEM_BYTES - 4 * MIB))


def _sigmoid(x):
    return 0.5 * jnp.tanh(0.5 * x) + 0.5


def _silu(x):
    h = 0.5 * x
    return h + h * jnp.tanh(h)


def _split2(a):
    hi = a.astype(BF16)
    lo = (a - hi.astype(F32)).astype(BF16)
    return hi, lo


def _split3(a):
    hi = a.astype(BF16)
    r = a - hi.astype(F32)
    mid = r.astype(BF16)
    lo = (r - mid.astype(F32)).astype(BF16)
    return hi, mid, lo


def _dot(a, b):
    return jnp.dot(a, b, preferred_element_type=F32)


def _dot_nt(a, b):
    return lax.dot_general(a, b, (((1,), (1,)), ((), ())), preferred_element_type=F32)


def _dot_tn(a, b):
    return lax.dot_general(a, b, (((0,), (0,)), ((), ())), preferred_element_type=F32)


def _ada_kernel(c_ref, w_ref, b_ref, o_ref):
    cond = _silu(c_ref[...])
    o_ref[...] = jnp.sum(cond * w_ref[...], axis=0, keepdims=True) + b_ref[...]


def _ada_mod(c, w_ada, b_ada, tn=1024):
    d, n = w_ada.shape
    return pl.pallas_call(
        _ada_kernel,
        grid=(n // tn,),
        in_specs=[pl.BlockSpec((d, 1), lambda j: (0, 0)),
                  pl.BlockSpec((d, tn), lambda j: (0, j)),
                  pl.BlockSpec((1, tn), lambda j: (0, j))],
        out_specs=pl.BlockSpec((1, tn), lambda j: (0, j)),
        out_shape=jax.ShapeDtypeStruct((1, n), F32),
        compiler_params=_cparams(("arbitrary",), _nbytes((d, tn), F32)),
        name="ada",
    )(c.reshape(d, 1), w_ada, b_ada.reshape(1, n))


def _norm_modulate(x, gain, scale, shift):
    y = x * lax.rsqrt(jnp.mean(x * x, axis=-1, keepdims=True) + EPS) * gain
    return y * (1.0 + scale) + shift


def _inproj_kernel(x_ref, g_ref, sc_ref, sh_ref, wm_ref, ws_ref, om_ref, os_ref, h_ref, *, q_scale):
    j = pl.program_id(1)

    @pl.when(j == 0)
    def _():
        h = _norm_modulate(x_ref[...], g_ref[...], sc_ref[...], sh_ref[...])
        h_b = h.astype(BF16)
        h_ref[...] = h_b
        w_hi, w_lo = _split2(ws_ref[...])
        both = _dot_nt(h_b, jnp.concatenate([w_hi, w_lo], axis=0))
        os_ref[...] = both[:, :LANES] + both[:, LANES:]

    acc = _dot_nt(h_ref[...], wm_ref[...])
    om_ref[...] = (acc * jnp.where(j == 0, q_scale, 1.0)).astype(BF16)


def _in_projection(x2, gain, scale, shift, w_main, w_small, tm=1024, tn=1024):
    t, d = x2.shape
    n = w_main.shape[0]
    row = lambda i, j: (i, 0)
    vec = lambda i, j: (0, 0)
    q_scale = HEAD_DIM ** -0.5 * LOG2E
    return pl.pallas_call(
        functools.partial(_inproj_kernel, q_scale=q_scale),
        grid=(t // tm, n // tn),
        in_specs=[pl.BlockSpec((tm, d), row),
                  pl.BlockSpec((1, d), vec), pl.BlockSpec((1, d), vec), pl.BlockSpec((1, d), vec),
                  pl.BlockSpec((tn, d), lambda i, j: (j, 0)),
                  pl.BlockSpec((LANES, d), vec)],
        out_specs=[pl.BlockSpec((tm, tn), lambda i, j: (i, j)),
                   pl.BlockSpec((tm, LANES), row)],
        out_shape=[jax.ShapeDtypeStruct((t, n), BF16), jax.ShapeDtypeStruct((t, LANES), F32)],
        scratch_shapes=[pltpu.VMEM((tm, d), BF16)],
        compiler_params=_cparams(("arbitrary", "arbitrary"),
                                 _nbytes((tm, d), F32) + _nbytes((tn, d), BF16) + _nbytes((LANES, d), F32)
                                 + _nbytes((tm, tn), BF16) + _nbytes((tm, LANES), F32),
                                 _nbytes((tm, d), BF16)),
        name="inproj",
    )(x2, gain, scale, shift, w_main, w_small)


def _foxpre_kernel(s_ref, fb_ref, kx_ref, cs_ref, carry_ref, *, tb):
    i = pl.program_id(0)

    @pl.when(i == 0)
    def _():
        carry_ref[...] = jnp.zeros_like(carry_ref)

    z = s_ref[...] + fb_ref[...]
    log_f = jnp.minimum(z, 0.0) - jnp.log1p(jnp.exp(-jnp.abs(z)))
    r = lax.broadcasted_iota(jnp.int32, (tb, tb), 0)
    c = lax.broadcasted_iota(jnp.int32, (tb, tb), 1)
    tri = jnp.where(c <= r, 1.0, 0.0).astype(BF16)
    hi, mid, lo = _split3(log_f)
    cum3 = _dot(tri, jnp.concatenate([hi, mid, lo], axis=1))
    cum = cum3[:, :LANES] + cum3[:, LANES:2 * LANES] + cum3[:, 2 * LANES:]
    carry = carry_ref[...]
    cs_ref[...] = jnp.broadcast_to(carry * LOG2E, cs_ref.shape)
    carry_ref[...] = carry + cum[tb - 1:tb, :]

    dd = -cum * LOG2E
    lane = lax.broadcasted_iota(jnp.int32, (tb, HEAD_DIM), 1)
    for h in range(HEADS):
        col = jnp.broadcast_to(dd[:, F_LANE + h:F_LANE + h + 1], (tb, HEAD_DIM))
        c_hi = col.astype(BF16).astype(F32)
        c_mid = (col - c_hi).astype(BF16).astype(F32)
        c_lo = col - c_hi - c_mid
        kx = jnp.where(lane == 0, c_hi, jnp.where(lane == 1, c_mid, jnp.where(lane == 2, c_lo, 0.0)))
        kx_ref[:, h * HEAD_DIM:(h + 1) * HEAD_DIM] = kx.astype(BF16)


def _fox_prologue(small, fbias_row, tb):
    t = small.shape[0]
    nblk = t // tb
    return pl.pallas_call(
        functools.partial(_foxpre_kernel, tb=tb),
        grid=(nblk,),
        in_specs=[pl.BlockSpec((tb, LANES), lambda i: (i, 0)),
                  pl.BlockSpec((1, LANES), lambda i: (0, 0))],
        out_specs=[pl.BlockSpec((tb, GROUP_W), lambda i: (i, 0)),
                   pl.BlockSpec((SUBLANES, LANES), lambda i: (i, 0))],
        out_shape=[jax.ShapeDtypeStruct((t, GROUP_W), BF16),
                   jax.ShapeDtypeStruct((nblk * SUBLANES, LANES), F32)],
        scratch_shapes=[pltpu.VMEM((1, LANES), F32)],
        compiler_params=_cparams(("arbitrary",), _nbytes((tb, LANES), F32) + _nbytes((tb, GROUP_W), BF16)),
        name="foxpre",
    )(small, fbias_row)


V_ROWS = HEAD_DIM + 16


def _fox_kernel(cs_ref, q_ref, k_ref, v_ref, kx_ref, gn_ref, o_ref,
                vt_ref, qt_ref, sa_ref, sb_ref, m_ref, acc_ref, *, tk, nblk):
    tq = 2 * tk
    h = pl.program_id(0)
    i = pl.program_id(1)

    @pl.when(i == 0)
    def _():
        row = lax.broadcasted_iota(jnp.int32, (V_ROWS - HEAD_DIM, tk), 0)
        tail = jnp.where(row == 0, 1.0, 0.0).astype(BF16)

        def fill(j, carry):
            off = pl.multiple_of(j * tk, tk)
            vt = v_ref[pl.ds(off, tk), :].astype(F32).T.astype(BF16)
            vt_ref[j] = jnp.concatenate([vt, tail], axis=0)
            return carry

        lax.fori_loop(0, nblk, fill, 0)

    row = lax.broadcasted_iota(jnp.int32, (HEAD_DIM, tq), 0)
    qt_ref[0:HEAD_DIM, :] = q_ref[...].astype(F32).T.astype(BF16)
    qt_ref[HEAD_DIM:2 * HEAD_DIM, :] = jnp.where(row < 3, 1.0, 0.0).astype(BF16)
    m_ref[...] = jnp.full_like(m_ref, NEG_INF)
    acc_ref[...] = jnp.zeros_like(acc_ref)
    first = 2 * i
    cq = cs_ref[first * HEADS + h]

    def scores(j, s_ref):
        off = pl.multiple_of(j * tk, tk)
        k_aug = jnp.concatenate([k_ref[pl.ds(off, tk), :], kx_ref[pl.ds(off, tk), :]], axis=1)
        s_ref[...] = _dot(k_aug, qt_ref[...])

    def softmax_pv(j, s_ref, key0=None):
        s = s_ref[...]
        if key0 is not None:
            key = lax.broadcasted_iota(jnp.int32, (tk, tq), 0) + key0
            qry = lax.broadcasted_iota(jnp.int32, (tk, tq), 1)
            s = jnp.where(key <= qry, s, NEG_INF)
        delta = cq - cs_ref[j * HEADS + h]
        m_old = m_ref[...]
        m_new = jnp.maximum(m_old, jnp.max(s, axis=0, keepdims=True) + delta)
        p = jnp.exp2(s - (m_new - delta))
        acc_ref[...] = acc_ref[...] * jnp.exp2(m_old - m_new) + _dot(vt_ref[j], p.astype(BF16))
        m_ref[...] = m_new

    scores(0, sa_ref)

    def pair_step(j0):
        scores(j0 + 1, sb_ref)
        softmax_pv(j0, sa_ref)
        scores(j0 + 2, sa_ref)
        softmax_pv(j0 + 1, sb_ref)

    def quad_body(jj, carry):
        pair_step(4 * jj)
        pair_step(4 * jj + 2)
        return carry

    lax.fori_loop(0, lax.shift_right_logical(i, 1), quad_body, 0)

    @pl.when(lax.bitwise_and(i, 1) == 1)
    def _():
        pair_step(first - 2)

    scores(first + 1, sb_ref)
    softmax_pv(first, sa_ref, key0=0)
    softmax_pv(first + 1, sb_ref, key0=tk)

    acc = acc_ref[...]
    o = acc[:HEAD_DIM, :] / acc[HEAD_DIM:HEAD_DIM + 1, :]
    o = o * lax.rsqrt(jnp.mean(o * o, axis=0, keepdims=True) + EPS) * gn_ref[...]
    o_ref[...] = o.T.astype(BF16)


def _fox_attention(cs_flat, main, kx, gnorm_col, tk):
    t = main.shape[0]
    tq = 2 * tk
    nblk = t // tk
    grid_spec = pltpu.PrefetchScalarGridSpec(
        num_scalar_prefetch=1,
        grid=(HEADS, t // tq),
        in_specs=[pl.BlockSpec((tq, HEAD_DIM), lambda h, i, cs: (i, h)),
                  pl.BlockSpec((t, HEAD_DIM), lambda h, i, cs: (0, HEADS + h)),
                  pl.BlockSpec((t, HEAD_DIM), lambda h, i, cs: (0, 2 * HEADS + h)),
                  pl.BlockSpec((t, HEAD_DIM), lambda h, i, cs: (0, h)),
                  pl.BlockSpec((HEAD_DIM, 1), lambda h, i, cs: (0, 0))],
        out_specs=pl.BlockSpec((tq, HEAD_DIM), lambda h, i, cs: (i, h)),
        scratch_shapes=[pltpu.VMEM((nblk, V_ROWS, tk), BF16), pltpu.VMEM((2 * HEAD_DIM, tq), BF16),
                        pltpu.VMEM((tk, tq), F32), pltpu.VMEM((tk, tq), F32),
                        pltpu.VMEM((1, tq), F32), pltpu.VMEM((V_ROWS, tq), F32)],
    )
    return pl.pallas_call(
        functools.partial(_fox_kernel, tk=tk, nblk=nblk),
        grid_spec=grid_spec,
        out_shape=jax.ShapeDtypeStruct((t, GROUP_W), BF16),
        compiler_params=_cparams(("arbitrary", "arbitrary"),
                                 3 * _nbytes((t, HEAD_DIM), BF16) + 2 * _nbytes((tq, HEAD_DIM), BF16),
                                 _nbytes((nblk, V_ROWS, tk), BF16) + _nbytes((2 * HEAD_DIM, tq), BF16)
                                 + 2 * _nbytes((tk, tq), F32) + _nbytes((V_ROWS + SUBLANES, tq), F32)),
        name="fox",
    )(cs_flat, main, main, main, kx, gnorm_col)


def _gdn_kernel(q_ref, k_ref, v_ref, gate_ref, s_ref, wq_ref, wk_ref, wv_ref, prm_ref, gn_ref, o_ref,
                xq_ref, xk_ref, xv_ref, yq_ref, yk_ref, yv_ref, state_ref, *, blk):
    i = pl.program_id(0)
    halo = SUBLANES

    @pl.when(i == 0)
    def _():
        state_ref[...] = jnp.zeros_like(state_ref)
        for x_ref in (xq_ref, xk_ref, xv_ref):
            x_ref[0:halo, :] = jnp.zeros((halo, GROUP_W), F32)

    def conv_silu(src_ref, x_ref, w_ref):
        x_ref[halo:halo + blk, :] = src_ref[...].astype(F32)
        y = w_ref[GDN_CONV - 1:GDN_CONV, :] * x_ref[halo:halo + blk, :]
        for tap in range(GDN_CONV - 1):
            shift = GDN_CONV - 1 - tap
            y = y + w_ref[tap:tap + 1, :] * x_ref[halo - shift:halo - shift + blk, :]
        x_ref[0:halo, :] = x_ref[blk:blk + halo, :]
        return _silu(y)

    def l2norm_heads(y, dst_ref, mult):
        for h in range(HEADS):
            yh = y[:, h * HEAD_DIM:(h + 1) * HEAD_DIM]
            inv = lax.rsqrt(jnp.sum(yh * yh, axis=-1, keepdims=True) + EPS) * mult
            dst_ref[:, h * HEAD_DIM:(h + 1) * HEAD_DIM] = yh * inv

    l2norm_heads(conv_silu(q_ref, xq_ref, wq_ref), yq_ref, HEAD_DIM ** -0.5)
    l2norm_heads(conv_silu(k_ref, xk_ref, wk_ref), yk_ref, 1.0)
    yv_ref[...] = conv_silu(v_ref, xv_ref, wv_ref)

    row = lax.broadcasted_iota(jnp.int32, (CHUNK, LANES), 0)
    lane = lax.broadcasted_iota(jnp.int32, (CHUNK, LANES), 1)
    left = lane < CHUNK
    lane_in = jnp.where(left, lane, lane - CHUNK)
    tri_incl = lane_in <= row
    tri_strict = lane_in < row
    eye_pair = jnp.where(lane_in == row, 1.0, 0.0)
    m_left = jnp.where(left, 1.0, 0.0).astype(BF16)
    m_right = jnp.where(left, 0.0, 1.0).astype(BF16)
    left_row = left[0:1, :]
    r2 = lax.broadcasted_iota(jnp.int32, (2 * CHUNK, CHUNK), 0)
    c2 = lax.broadcasted_iota(jnp.int32, (2 * CHUNK, CHUNK), 1)
    tri_twice = jnp.where(c2 <= jnp.where(r2 < CHUNK, r2, r2 - CHUNK), 1.0, 0.0).astype(BF16)
    dt_bias = prm_ref[0:1, :]
    neg_a = -jnp.exp(prm_ref[1:2, :])
    gn = gn_ref[...]

    def blockdiag(x_b):
        return jnp.concatenate([x_b * m_left, x_b * m_right], axis=0)

    def split_bd(x):
        hi, lo = _split2(x)
        return hi, lo, blockdiag(hi), blockdiag(lo)

    def pair_dot3(a_hi, a_lo, b_hi_bd, b_lo_bd):
        return _dot(jnp.concatenate([a_hi, a_hi, a_lo], axis=1),
                    jnp.concatenate([b_hi_bd, b_lo_bd, b_hi_bd], axis=0))

    def blockdiag2(a, b):
        z = jnp.zeros_like(a)
        return jnp.concatenate([jnp.concatenate([a, z], axis=1), jnp.concatenate([z, b], axis=1)], axis=0)

    def group_body(gi, carry):
        nc = CHUNKS_PER_STEP
        rows = [pl.ds(pl.multiple_of((gi * nc + c) * CHUNK, CHUNK), CHUNK) for c in range(nc)]
        gc, gc2_t, beta = [], [], []
        for c in range(nc):
            slab = s_ref[rows[c], :]
            z = slab + dt_bias
            g = neg_a * (jnp.maximum(z, 0.0) + jnp.log1p(jnp.exp(-jnp.abs(z))))
            beta.append(_sigmoid(slab))
            g_hi, g_mid, g_lo = _split3(g)
            gc3 = _dot(tri_twice, jnp.concatenate([g_hi, g_mid, g_lo], axis=1))
            gc2 = gc3[:, :LANES] + gc3[:, LANES:2 * LANES] + gc3[:, 2 * LANES:]
            gc.append(gc2[:CHUNK])
            gc2_t.append(gc2.T)

        slots = [(c, h) for c in range(nc) for h in range(HEADS)]
        cols = [slice(h * HEAD_DIM, (h + 1) * HEAD_DIM) for h in range(HEADS)]
        gcol = [gc[c][:, A_LANE + h:A_LANE + h + 1] for c, h in slots]
        g_last = [gc[c][CHUNK - 1:CHUNK, A_LANE + h:A_LANE + h + 1] for c, h in slots]
        e_gc = [jnp.exp(x) for x in gcol]
        e_tail = [jnp.exp(gl - x) for gl, x in zip(g_last, gcol)]
        bcol = [beta[c][:, B_LANE + h:B_LANE + h + 1] for c, h in slots]
        q = [yq_ref[rows[c], cols[h]] for c, h in slots]
        k = [yk_ref[rows[c], cols[h]] for c, h in slots]
        kb = [x * b for x, b in zip(k, bcol)]
        vb = [yv_ref[rows[c], cols[h]] * bcol[c * HEADS + h] for c, h in slots]
        k_b = [x.astype(BF16) for x in k]
        pairs = [(c, c * HEADS + 2 * p, c * HEADS + 2 * p + 1) for c in range(nc) for p in range(HEADS // 2)]

        decay, low, attn = [], [], []
        for c, a, b in pairs:
            ha, hb = a - c * HEADS, b - c * HEADS
            diff = (jnp.where(left, gcol[a], gcol[b])
                    - jnp.where(left_row, gc2_t[c][A_LANE + ha:A_LANE + ha + 1, :],
                                gc2_t[c][A_LANE + hb:A_LANE + hb + 1, :]))
            decay.append(jnp.exp(jnp.where(tri_incl, diff, NEG_INF)))
        for p, (c, a, b) in enumerate(pairs):
            lhs = jnp.concatenate([jnp.concatenate([kb[a], kb[b]], axis=1),
                                   jnp.concatenate([q[a], q[b]], axis=1)], axis=0).astype(BF16)
            kk_qk = _dot_nt(lhs, blockdiag2(k_b[a], k_b[b]))
            low.append(jnp.where(tri_strict, kk_qk[:CHUNK] * decay[p], 0.0))
            attn.append((kk_qk[CHUNK:] * decay[p]).astype(BF16))

        power = [pair_dot3(*split_bd(-l)) for l in low]
        inv = [eye_pair - l for l in low]
        for _ in range(int(math.log2(CHUNK)) - 2):
            both = []
            for s, pw in zip(inv, power):
                hi, lo = _split2(jnp.concatenate([s, pw], axis=0))
                both.append(pair_dot3(hi, lo, blockdiag(hi[CHUNK:]), blockdiag(lo[CHUNK:])))
            inv = [s + r[:CHUNK] for s, r in zip(inv, both)]
            power = [r[CHUNK:] for r in both]
        inv = [s + pair_dot3(*_split2(s), *split_bd(pw)[2:]) for s, pw in zip(inv, power)]

        u, w = [None] * len(slots), [None] * len(slots)
        for p, (c, a, b) in enumerate(pairs):
            xa_hi, xa_lo = _split2(jnp.concatenate([vb[a], kb[a] * e_gc[a]], axis=1))
            xb_hi, xb_lo = _split2(jnp.concatenate([vb[b], kb[b] * e_gc[b]], axis=1))
            x_hi = blockdiag2(xa_hi, xb_hi)
            t_hi, t_lo = _split2(inv[p])
            sol = _dot(jnp.concatenate([t_hi, t_hi, t_lo], axis=1),
                       jnp.concatenate([x_hi, blockdiag2(xa_lo, xb_lo), x_hi], axis=0))
            u[a], w[a], u[b], w[b] = (sol[:, n * HEAD_DIM:(n + 1) * HEAD_DIM] for n in range(4))
        wq = [jnp.concatenate([w[s], q[s] * e_gc[s]], axis=0).astype(BF16) for s in range(len(slots))]
        k_tail = [(k[s] * e_tail[s]).astype(BF16) for s in range(len(slots))]
        e_last = [jnp.exp(x) for x in g_last]

        for c in range(nc):
            base = c * HEADS
            state = [state_ref[h] for h in range(HEADS)]
            ws_qs = [_dot(wq[base + h], state[h].astype(BF16)) for h in range(HEADS)]
            v_new = [(u[base + h] - ws_qs[h][:CHUNK]).astype(BF16) for h in range(HEADS)]
            for h in range(HEADS):
                state_ref[h] = state[h] * e_last[base + h] + _dot_tn(k_tail[base + h], v_new[h])
            for p in range(HEADS // 2):
                a, b = 2 * p, 2 * p + 1
                o_pair = _dot(attn[c * (HEADS // 2) + p], blockdiag2(v_new[a], v_new[b]))
                for h, o_mm in ((a, o_pair[:, :HEAD_DIM]), (b, o_pair[:, HEAD_DIM:])):
                    o = ws_qs[h][CHUNK:] + o_mm
                    gate = gate_ref[rows[c], cols[h]].astype(F32)
                    on = o * lax.rsqrt(jnp.mean(o * o, axis=-1, keepdims=True) + EPS) * gn * _silu(gate)
                    o_ref[rows[c], cols[h]] = on.astype(BF16)
        return carry

    lax.fori_loop(0, blk // (CHUNK * CHUNKS_PER_STEP), group_body, 0)


def _gated_deltanet(main, small, conv_w, prm, gnorm, blk=512):
    t = main.shape[0]
    qcol = 3 * GROUP_W // GROUP_W
    big = lambda cb: pl.BlockSpec((blk, GROUP_W), lambda i: (i, cb))
    wspec = lambda cb: pl.BlockSpec((GDN_CONV, GROUP_W), lambda i: (0, cb))
    small_vec = lambda rows: pl.BlockSpec((rows, LANES), lambda i: (0, 0))
    return pl.pallas_call(
        functools.partial(_gdn_kernel, blk=blk),
        grid=(t // blk,),
        in_specs=[big(qcol), big(qcol + 1), big(qcol + 2), big(qcol + 3),
                  pl.BlockSpec((blk, LANES), lambda i: (i, 0)),
                  wspec(0), wspec(1), wspec(2),
                  small_vec(SUBLANES), small_vec(1)],
        out_specs=pl.BlockSpec((blk, GROUP_W), lambda i: (i, 0)),
        out_shape=jax.ShapeDtypeStruct((t, GROUP_W), BF16),
        scratch_shapes=[pltpu.VMEM((blk + SUBLANES, GROUP_W), F32)] * 3
                       + [pltpu.VMEM((blk, GROUP_W), F32)] * 3
                       + [pltpu.VMEM((HEADS, HEAD_DIM, HEAD_DIM), F32)],
        compiler_params=_cparams(("arbitrary",),
                                 5 * _nbytes((blk, GROUP_W), BF16) + _nbytes((blk, LANES), F32),
                                 6 * _nbytes((blk + SUBLANES, GROUP_W), F32)),
        name="gdn",
    )(main, main, main, main, small, conv_w, conv_w, conv_w, prm, gnorm)


def _outproj_kernel(oa_ref, ob_ref, w_ref, x_ref, g_ref, ga_ref, o_ref):
    y = _dot(oa_ref[...], w_ref[0:GROUP_W, :]) + _dot(ob_ref[...], w_ref[GROUP_W:2 * GROUP_W, :])
    yn = y * lax.rsqrt(jnp.mean(y * y, axis=-1, keepdims=True) + EPS) * g_ref[...]
    o_ref[...] = x_ref[...] + ga_ref[...] * yn


def _out_projection(o_a, o_b, w_out, x2, gain, gate, tm=512):
    t, d = x2.shape
    row = lambda i: (i, 0)
    vec = lambda i: (0, 0)
    return pl.pallas_call(
        _outproj_kernel,
        grid=(t // tm,),
        in_specs=[pl.BlockSpec((tm, GROUP_W), row), pl.BlockSpec((tm, GROUP_W), row),
                  pl.BlockSpec((2 * GROUP_W, d), vec),
                  pl.BlockSpec((tm, d), row),
                  pl.BlockSpec((1, d), vec), pl.BlockSpec((1, d), vec)],
        out_specs=pl.BlockSpec((tm, d), row),
        out_shape=jax.ShapeDtypeStruct((t, d), F32),
        compiler_params=_cparams(("arbitrary",),
                                 2 * _nbytes((tm, GROUP_W), BF16) + _nbytes((2 * GROUP_W, d), BF16)
                                 + 2 * _nbytes((tm, d), F32)),
        name="outproj",
    )(o_a, o_b, w_out, x2, gain, gate)


def _ffn_kernel(x_ref, g_ref, sc_ref, sh_ref, wg_ref, wv_ref, cwg_ref, cwv_ref, cbg_ref, cbv_ref, wd_ref,
                pg_ref, ga_ref, o_ref, h_ref, acc_ref, ug_ref, uv_ref, cg_ref, cv_ref, *, tm, nf):
    i = pl.program_id(0)
    j = pl.program_id(1)
    halo = SUBLANES

    @pl.when(j == 0)
    def _():
        h_ref[...] = _norm_modulate(x_ref[...], g_ref[...], sc_ref[...], sh_ref[...]).astype(BF16)
        acc_ref[...] = jnp.zeros_like(acc_ref)

    @pl.when(i == 0)
    def _():
        cg_ref[j] = jnp.zeros(cg_ref.shape[1:], F32)
        cv_ref[j] = jnp.zeros(cv_ref.shape[1:], F32)

    def up_conv(w_ref, u_ref, carry_ref, cw_ref, cb_ref):
        u = _dot(h_ref[...], w_ref[...])
        u_ref[0:halo, :] = carry_ref[j]
        u_ref[halo:halo + tm, :] = u
        carry_ref[j] = u[tm - halo:tm, :]
        y = cw_ref[FFN_CONV - 1:FFN_CONV, :] * u + cb_ref[...]
        for tap in range(FFN_CONV - 1):
            shift = FFN_CONV - 1 - tap
            y = y + cw_ref[tap:tap + 1, :] * u_ref[halo - shift:halo - shift + tm, :]
        return y

    gate = up_conv(wg_ref, ug_ref, cg_ref, cwg_ref, cbg_ref)
    val = up_conv(wv_ref, uv_ref, cv_ref, cwv_ref, cbv_ref)
    act = (gate * (1.0 / (1.0 + jnp.exp(-gate))) * val).astype(BF16)
    acc_ref[...] += _dot(act, wd_ref[...])

    @pl.when(j == nf - 1)
    def _():
        y = acc_ref[...]
        yn = y * lax.rsqrt(jnp.mean(y * y, axis=-1, keepdims=True) + EPS) * pg_ref[...]
        o_ref[...] = x_ref[...] + ga_ref[...] * yn


def _conv_ffn(x2, gain, scale, shift, w_up, conv_w, conv_b, w_down, post_gain, gate, tm=512, tf=512):
    t, d = x2.shape
    d_ff = w_down.shape[0]
    nf = d_ff // tf
    row = lambda i, j: (i, 0)
    vec = lambda i, j: (0, 0)
    gcol = lambda i, j: (0, j)
    vcol = lambda i, j: (0, nf + j)
    return pl.pallas_call(
        functools.partial(_ffn_kernel, tm=tm, nf=nf),
        grid=(t // tm, nf),
        in_specs=[pl.BlockSpec((tm, d), row),
                  pl.BlockSpec((1, d), vec), pl.BlockSpec((1, d), vec), pl.BlockSpec((1, d), vec),
                  pl.BlockSpec((d, tf), gcol), pl.BlockSpec((d, tf), vcol),
                  pl.BlockSpec((FFN_CONV, tf), gcol), pl.BlockSpec((FFN_CONV, tf), vcol),
                  pl.BlockSpec((1, tf), gcol), pl.BlockSpec((1, tf), vcol),
                  pl.BlockSpec((tf, d), lambda i, j: (j, 0)),
                  pl.BlockSpec((1, d), vec), pl.BlockSpec((1, d), vec)],
        out_specs=pl.BlockSpec((tm, d), row),
        out_shape=jax.ShapeDtypeStruct((t, d), F32),
        scratch_shapes=[pltpu.VMEM((tm, d), BF16), pltpu.VMEM((tm, d), F32),
                        pltpu.VMEM((tm + SUBLANES, tf), F32), pltpu.VMEM((tm + SUBLANES, tf), F32),
                        pltpu.VMEM((nf, SUBLANES, tf), F32), pltpu.VMEM((nf, SUBLANES, tf), F32)],
        compiler_params=_cparams(("arbitrary", "arbitrary"),
                                 2 * _nbytes((tm, d), F32) + 3 * _nbytes((d, tf), BF16),
                                 _nbytes((tm, d), BF16) + _nbytes((tm, d), F32)
                                 + 2 * _nbytes((tm + SUBLANES, tf), F32) + 2 * _nbytes((nf, SUBLANES, tf), F32)),
        name="ffn",
    )(x2, gain, scale, shift, w_up, w_up, conv_w, conv_w, conv_b, conv_b, w_down, post_gain, gate)


def _lane_row(vals, lane0):
    return jnp.zeros((1, LANES), F32).at[0, lane0:lane0 + HEADS].set(vals.astype(F32))


def _layer(x2, cond, p, attn_block):
    d = x2.shape[1]
    mod = _ada_mod(cond, p["w_ada"], p["b_ada"])
    sh_m, sc_m, ga_m, sh_f, sc_f, ga_f = (mod[:, k * d:(k + 1) * d] for k in range(6))

    w_in = p["w_in"]
    o_f = 3 * GROUP_W
    o_g = o_f + HEADS
    o_a = o_g + 3 * GROUP_W
    o_gate = o_a + 2 * HEADS
    w_t = w_in.T
    w_main = jnp.concatenate([w_t[:o_f].astype(BF16), w_t[o_g:o_a].astype(BF16), w_t[o_gate:].astype(BF16)], axis=0)
    w_small = jnp.concatenate([w_t[o_f:o_g], w_t[o_a:o_gate], jnp.zeros((LANES - 3 * HEADS, d), F32)], axis=0)

    main, small = _in_projection(x2, p["norm_mix_pre"].reshape(1, d), sc_m, sh_m, w_main, w_small)

    kx, cs = _fox_prologue(small, _lane_row(p["fox_forget_bias"], F_LANE), attn_block)
    nblk = x2.shape[0] // attn_block
    cs_flat = cs.reshape(nblk, SUBLANES, LANES)[:, 0, F_LANE:F_LANE + HEADS].reshape(-1)
    o_fox = _fox_attention(cs_flat, main, kx, p["fox_out_norm"].reshape(HEAD_DIM, 1), attn_block)

    prm = jnp.concatenate([_lane_row(p["gdn_dt_bias"], A_LANE), _lane_row(p["gdn_A_log"], A_LANE),
                           jnp.zeros((SUBLANES - 2, LANES), F32)], axis=0)
    o_gdn = _gated_deltanet(main, small, p["gdn_conv_w"], prm, p["gdn_out_norm"].reshape(1, HEAD_DIM))

    x2 = _out_projection(o_fox, o_gdn, p["w_out"].astype(BF16), x2, p["norm_mix_post"].reshape(1, d), ga_m)

    return _conv_ffn(x2, p["norm_ffn_pre"].reshape(1, d), sc_f, sh_f, p["w_up"].astype(BF16),
                     p["ffn_conv_w"], p["ffn_conv_b"].reshape(1, -1), p["w_down"].astype(BF16),
                     p["norm_ffn_post"].reshape(1, d), ga_f)


def kernel(x, c, w_ada, b_ada, norm_mix_pre, norm_mix_post, w_in, fox_forget_bias, fox_out_norm, gdn_conv_w, gdn_A_log, gdn_dt_bias, gdn_out_norm, w_out, norm_ffn_pre, norm_ffn_post, w_up, ffn_conv_w, ffn_conv_b, w_down):
    b, t, d = x.shape
    assert b == 1, "single-sequence kernel"
    params = dict(w_ada=w_ada, b_ada=b_ada, norm_mix_pre=norm_mix_pre, norm_mix_post=norm_mix_post, w_in=w_in,
                  fox_forget_bias=fox_forget_bias, fox_out_norm=fox_out_norm, gdn_conv_w=gdn_conv_w,
                  gdn_A_log=gdn_A_log, gdn_dt_bias=gdn_dt_bias, gdn_out_norm=gdn_out_norm, w_out=w_out,
                  norm_ffn_pre=norm_ffn_pre, norm_ffn_post=norm_ffn_post, w_up=w_up, ffn_conv_w=ffn_conv_w,
                  ffn_conv_b=ffn_conv_b, w_down=w_down)
    attn_block = min(512, t)
    x2 = x.reshape(t, d)
    for l in range(w_ada.shape[0]):
        x2 = _layer(x2, c, {k: v[l] for k, v in params.items()}, attn_block)
    return x2.reshape(b, t, d)
```

```python
import functools
import math

import jax
import jax.numpy as jnp
from jax import lax
from jax.experimental import pallas as pl
from jax.experimental.pallas import tpu as pltpu

F32 = jnp.float32
BF16 = jnp.bfloat16

EPS = 1e-6
HEADS = 8
HEAD_DIM = 128
CHUNK = 64
CHUNKS_PER_STEP = 4
GDN_CONV = 4
FFN_CONV = 3
LANES = 128
SUBLANES = 8
GROUP_W = HEADS * HEAD_DIM
LOG2E = math.log2(math.e)
NEG_INF = float("-inf")

F_LANE, A_LANE, B_LANE = 0, HEADS, 2 * HEADS


MIB = 1024 * 1024
V7X_VMEM_BYTES = 64 * MIB
TEMP_ALLOWANCE_BYTES = 22 * MIB


def _nbytes(shape, dtype):
    return math.prod(shape) * jnp.dtype(dtype).itemsize


def _cparams(semantics, window_bytes, scratch_bytes=0, flags=None):
    want = 2 * window_bytes + scratch_bytes + TEMP_ALLOWANCE_BYTES
    return pltpu.CompilerParams(dimension_semantics=semantics, flags=flags,
                                vmem_limit_bytes=min(want, V7X_VMEM_BYTES - 4 * MIB))


def _sigmoid(x):
    return 0.5 * jnp.tanh(0.5 * x) + 0.5


def _silu(x):
    h = 0.5 * x
    return h + h * jnp.tanh(h)


def _split2(a):
    hi = a.astype(BF16)
    lo = (a - hi.astype(F32)).astype(BF16)
    return hi, lo


def _split3(a):
    hi = a.astype(BF16)
    r = a - hi.astype(F32)
    mid = r.astype(BF16)
    lo = (r - mid.astype(F32)).astype(BF16)
    return hi, mid, lo


def _dot(a, b):
    return jnp.dot(a, b, preferred_element_type=F32)


def _dot_nt(a, b):
    return lax.dot_general(a, b, (((1,), (1,)), ((), ())), preferred_element_type=F32)


def _dot_tn(a, b):
    return lax.dot_general(a, b, (((0,), (0,)), ((), ())), preferred_element_type=F32)


def _ada_kernel(c_ref, w_ref, b_ref, o_ref):
    cond = _silu(c_ref[...])
    o_ref[...] = jnp.sum(cond * w_ref[...], axis=0, keepdims=True) + b_ref[...]


def _ada_mod(c, w_ada, b_ada, tn=1024):
    d, n = w_ada.shape
    return pl.pallas_call(
        _ada_kernel,
        grid=(n // tn,),
        in_specs=[pl.BlockSpec((d, 1), lambda j: (0, 0)),
                  pl.BlockSpec((d, tn), lambda j: (0, j)),
                  pl.BlockSpec((1, tn), lambda j: (0, j))],
        out_specs=pl.BlockSpec((1, tn), lambda j: (0, j)),
        out_shape=jax.ShapeDtypeStruct((1, n), F32),
        compiler_params=_cparams(("arbitrary",), _nbytes((d, tn), F32)),
        name="ada",
    )(c.reshape(d, 1), w_ada, b_ada.reshape(1, n))


def _norm_modulate(x, gain, scale, shift):
    y = x * lax.rsqrt(jnp.mean(x * x, axis=-1, keepdims=True) + EPS) * gain
    return y * (1.0 + scale) + shift


def _pack_kernel(w_ref, o_ref):
    o_ref[...] = w_ref[...].astype(BF16)


def _pack_rows(w_t, segments, tr=1024):
    d = w_t.shape[1]
    shifts, out_rows = [], 0
    for start, size in segments:
        shifts += [start - out_rows] * (size // tr)
        out_rows += size

    assert tr % SUBLANES == 0 and all(s % SUBLANES == 0 for s in shifts)

    def src_row(j):
        off = jnp.int32(shifts[0] // SUBLANES)
        for b in range(1, len(shifts)):
            if shifts[b] != shifts[b - 1]:
                off = jnp.where(j >= b, jnp.int32(shifts[b] // SUBLANES), off)
        return (j * (tr // SUBLANES) + off) * SUBLANES

    return pl.pallas_call(
        _pack_kernel,
        grid=(out_rows // tr,),
        in_specs=[pl.BlockSpec((pl.Element(tr), pl.Element(d)), lambda j: (src_row(j), 0))],
        out_specs=pl.BlockSpec((tr, d), lambda j: (j, 0)),
        out_shape=jax.ShapeDtypeStruct((out_rows, d), BF16),
        compiler_params=_cparams(("arbitrary",), _nbytes((tr, d), F32) + _nbytes((tr, d), BF16)),
        name="pack",
    )(w_t)


def _inproj_kernel(x_ref, g_ref, sc_ref, sh_ref, wm_ref, ws_ref, om_ref, os_ref, h_ref, *, q_scale):
    j = pl.program_id(1)

    @pl.when(j == 0)
    def _():
        h = _norm_modulate(x_ref[...], g_ref[...], sc_ref[...], sh_ref[...])
        h_b = h.astype(BF16)
        h_ref[...] = h_b
        w_hi, w_lo = _split2(ws_ref[...])
        both = _dot_nt(h_b, jnp.concatenate([w_hi, w_lo], axis=0))
        os_ref[...] = both[:, :LANES] + both[:, LANES:]

    acc = _dot_nt(h_ref[...], wm_ref[...])
    om_ref[...] = (acc * jnp.where(j == 0, q_scale, 1.0)).astype(BF16)


def _in_projection(x2, gain, scale, shift, w_main, w_small, tm=1024, tn=1024):
    t, d = x2.shape
    n = w_main.shape[0]
    row = lambda i, j: (i, 0)
    vec = lambda i, j: (0, 0)
    q_scale = HEAD_DIM ** -0.5 * LOG2E
    return pl.pallas_call(
        functools.partial(_inproj_kernel, q_scale=q_scale),
        grid=(t // tm, n // tn),
        in_specs=[pl.BlockSpec((tm, d), row),
                  pl.BlockSpec((1, d), vec), pl.BlockSpec((1, d), vec), pl.BlockSpec((1, d), vec),
                  pl.BlockSpec((tn, d), lambda i, j: (j, 0)),
                  pl.BlockSpec((LANES, d), vec)],
        out_specs=[pl.BlockSpec((tm, tn), lambda i, j: (i, j)),
                   pl.BlockSpec((tm, LANES), row)],
        out_shape=[jax.ShapeDtypeStruct((t, n), BF16), jax.ShapeDtypeStruct((t, LANES), F32)],
        scratch_shapes=[pltpu.VMEM((tm, d), BF16)],
        compiler_params=_cparams(("arbitrary", "arbitrary"),
                                 _nbytes((tm, d), F32) + _nbytes((tn, d), BF16) + _nbytes((LANES, d), F32)
                                 + _nbytes((tm, tn), BF16) + _nbytes((tm, LANES), F32),
                                 _nbytes((tm, d), BF16)),
        name="inproj",
    )(x2, gain, scale, shift, w_main, w_small)


def _foxpre_kernel(s_ref, fb_ref, kx_ref, cs_ref, carry_ref, *, tb):
    i = pl.program_id(0)

    @pl.when(i == 0)
    def _():
        carry_ref[...] = jnp.zeros_like(carry_ref)

    z = s_ref[...] + fb_ref[...]
    log_f = jnp.minimum(z, 0.0) - jnp.log1p(jnp.exp(-jnp.abs(z)))
    r = lax.broadcasted_iota(jnp.int32, (tb, tb), 0)
    c = lax.broadcasted_iota(jnp.int32, (tb, tb), 1)
    tri = jnp.where(c <= r, 1.0, 0.0).astype(BF16)
    hi, mid, lo = _split3(log_f)
    cum3 = _dot(tri, jnp.concatenate([hi, mid, lo], axis=1))
    cum = cum3[:, :LANES] + cum3[:, LANES:2 * LANES] + cum3[:, 2 * LANES:]
    carry = carry_ref[...]
    cs_ref[...] = jnp.broadcast_to(carry * LOG2E, cs_ref.shape)
    carry_ref[...] = carry + cum[tb - 1:tb, :]

    dd = -cum * LOG2E
    lane = lax.broadcasted_iota(jnp.int32, (tb, HEAD_DIM), 1)
    for h in range(HEADS):
        col = jnp.broadcast_to(dd[:, F_LANE + h:F_LANE + h + 1], (tb, HEAD_DIM))
        c_hi = col.astype(BF16).astype(F32)
        c_mid = (col - c_hi).astype(BF16).astype(F32)
        c_lo = col - c_hi - c_mid
        kx = jnp.where(lane == 0, c_hi, jnp.where(lane == 1, c_mid, jnp.where(lane == 2, c_lo, 0.0)))
        kx_ref[:, h * HEAD_DIM:(h + 1) * HEAD_DIM] = kx.astype(BF16)


def _fox_prologue(small, fbias_row, tb):
    t = small.shape[0]
    nblk = t // tb
    return pl.pallas_call(
        functools.partial(_foxpre_kernel, tb=tb),
        grid=(nblk,),
        in_specs=[pl.BlockSpec((tb, LANES), lambda i: (i, 0)),
                  pl.BlockSpec((1, LANES), lambda i: (0, 0))],
        out_specs=[pl.BlockSpec((tb, GROUP_W), lambda i: (i, 0)),
                   pl.BlockSpec((SUBLANES, LANES), lambda i: (i, 0))],
        out_shape=[jax.ShapeDtypeStruct((t, GROUP_W), BF16),
                   jax.ShapeDtypeStruct((nblk * SUBLANES, LANES), F32)],
        scratch_shapes=[pltpu.VMEM((1, LANES), F32)],
        compiler_params=_cparams(("arbitrary",), _nbytes((tb, LANES), F32) + _nbytes((tb, GROUP_W), BF16)),
        name="foxpre",
    )(small, fbias_row)


V_ROWS = HEAD_DIM + 16


def _fox_kernel(cs_ref, q_ref, k_ref, v_ref, kx_ref, gn_ref, o_ref,
                vt_ref, qt_ref, sa_ref, sb_ref, m_ref, acc_ref, *, tk, nblk):
    tq = 2 * tk
    h = pl.program_id(0)
    i = pl.program_id(1)

    @pl.when(i == 0)
    def _():
        row = lax.broadcasted_iota(jnp.int32, (V_ROWS - HEAD_DIM, tk), 0)
        tail = jnp.where(row == 0, 1.0, 0.0).astype(BF16)

        def fill(j, carry):
            off = pl.multiple_of(j * tk, tk)
            vt = v_ref[pl.ds(off, tk), :].astype(F32).T.astype(BF16)
            vt_ref[j] = jnp.concatenate([vt, tail], axis=0)
            return carry

        lax.fori_loop(0, nblk, fill, 0)

    row = lax.broadcasted_iota(jnp.int32, (HEAD_DIM, tq), 0)
    qt_ref[0:HEAD_DIM, :] = q_ref[...].astype(F32).T.astype(BF16)
    qt_ref[HEAD_DIM:2 * HEAD_DIM, :] = jnp.where(row < 3, 1.0, 0.0).astype(BF16)
    m_ref[...] = jnp.full_like(m_ref, NEG_INF)
    acc_ref[...] = jnp.zeros_like(acc_ref)
    first = 2 * i
    cq = cs_ref[first * HEADS + h]

    def scores(j, s_ref, q0=0):
        off = pl.multiple_of(j * tk, tk)
        k_aug = jnp.concatenate([k_ref[pl.ds(off, tk), :], kx_ref[pl.ds(off, tk), :]], axis=1)
        s_ref[:, q0:] = _dot(k_aug, qt_ref[:, q0:])

    def softmax_pv(j, s_ref, masked=False, q0=0):
        s = s_ref[:, q0:]
        if masked:
            key = lax.broadcasted_iota(jnp.int32, s.shape, 0)
            qry = lax.broadcasted_iota(jnp.int32, s.shape, 1)
            s = jnp.where(key <= qry, s, NEG_INF)
        delta = cq - cs_ref[j * HEADS + h]
        m_old = m_ref[:, q0:]
        m_new = jnp.maximum(m_old, jnp.max(s, axis=0, keepdims=True) + delta)
        p = jnp.exp2(s - (m_new - delta))
        acc_ref[:, q0:] = acc_ref[:, q0:] * jnp.exp2(m_old - m_new) + _dot(vt_ref[j], p.astype(BF16))
        m_ref[:, q0:] = m_new

    scores(0, sa_ref)

    def pair_step(j0):
        scores(j0 + 1, sb_ref)
        softmax_pv(j0, sa_ref)
        scores(j0 + 2, sa_ref)
        softmax_pv(j0 + 1, sb_ref)

    def quad_body(jj, carry):
        pair_step(4 * jj)
        pair_step(4 * jj + 2)
        return carry

    lax.fori_loop(0, lax.shift_right_logical(i, 1), quad_body, 0)

    @pl.when(lax.bitwise_and(i, 1) == 1)
    def _():
        pair_step(first - 2)

    scores(first + 1, sb_ref, q0=tk)
    softmax_pv(first, sa_ref, masked=True)
    softmax_pv(first + 1, sb_ref, masked=True, q0=tk)

    acc = acc_ref[...]
    o = acc[:HEAD_DIM, :] / acc[HEAD_DIM:HEAD_DIM + 1, :]
    o = o * lax.rsqrt(jnp.mean(o * o, axis=0, keepdims=True) + EPS) * gn_ref[...]
    o_ref[...] = o.T.astype(BF16)


def _fox_attention(cs_flat, main, kx, gnorm_col, tk):
    t = main.shape[0]
    tq = 2 * tk
    nblk = t // tk
    grid_spec = pltpu.PrefetchScalarGridSpec(
        num_scalar_prefetch=1,
        grid=(HEADS, t // tq),
        in_specs=[pl.BlockSpec((tq, HEAD_DIM), lambda h, i, cs: (i, h)),
                  pl.BlockSpec((t, HEAD_DIM), lambda h, i, cs: (0, HEADS + h)),
                  pl.BlockSpec((t, HEAD_DIM), lambda h, i, cs: (0, 2 * HEADS + h)),
                  pl.BlockSpec((t, HEAD_DIM), lambda h, i, cs: (0, h)),
                  pl.BlockSpec((HEAD_DIM, 1), lambda h, i, cs: (0, 0))],
        out_specs=pl.BlockSpec((tq, HEAD_DIM), lambda h, i, cs: (i, h)),
        scratch_shapes=[pltpu.VMEM((nblk, V_ROWS, tk), BF16), pltpu.VMEM((2 * HEAD_DIM, tq), BF16),
                        pltpu.VMEM((tk, tq), F32), pltpu.VMEM((tk, tq), F32),
                        pltpu.VMEM((1, tq), F32), pltpu.VMEM((V_ROWS, tq), F32)],
    )
    return pl.pallas_call(
        functools.partial(_fox_kernel, tk=tk, nblk=nblk),
        grid_spec=grid_spec,
        out_shape=jax.ShapeDtypeStruct((t, GROUP_W), BF16),
        compiler_params=_cparams(("arbitrary", "arbitrary"),
                                 3 * _nbytes((t, HEAD_DIM), BF16) + 2 * _nbytes((tq, HEAD_DIM), BF16),
                                 _nbytes((nblk, V_ROWS, tk), BF16) + _nbytes((2 * HEAD_DIM, tq), BF16)
                                 + 2 * _nbytes((tk, tq), F32) + _nbytes((V_ROWS + SUBLANES, tq), F32)),
        name="fox",
    )(cs_flat, main, main, main, kx, gnorm_col)


def _gdn_kernel(q_ref, k_ref, v_ref, gate_ref, s_ref, wq_ref, wk_ref, wv_ref, prm_ref, gn_ref, o_ref,
                xq_ref, xk_ref, xv_ref, yq_ref, yk_ref, yv_ref, state_ref, *, blk):
    i = pl.program_id(0)
    halo = SUBLANES

    @pl.when(i == 0)
    def _():
        state_ref[...] = jnp.zeros_like(state_ref)
        for x_ref in (xq_ref, xk_ref, xv_ref):
            x_ref[0:halo, :] = jnp.zeros((halo, GROUP_W), F32)

    def conv_silu(src_ref, x_ref, w_ref):
        x_ref[halo:halo + blk, :] = src_ref[...].astype(F32)
        y = w_ref[GDN_CONV - 1:GDN_CONV, :] * x_ref[halo:halo + blk, :]
        for tap in range(GDN_CONV - 1):
            shift = GDN_CONV - 1 - tap
            y = y + w_ref[tap:tap + 1, :] * x_ref[halo - shift:halo - shift + blk, :]
        x_ref[0:halo, :] = x_ref[blk:blk + halo, :]
        return _silu(y)

    def l2norm_heads(y, dst_ref, mult):
        for h in range(HEADS):
            yh = y[:, h * HEAD_DIM:(h + 1) * HEAD_DIM]
            inv = lax.rsqrt(jnp.sum(yh * yh, axis=-1, keepdims=True) + EPS) * mult
            dst_ref[:, h * HEAD_DIM:(h + 1) * HEAD_DIM] = yh * inv

    l2norm_heads(conv_silu(q_ref, xq_ref, wq_ref), yq_ref, HEAD_DIM ** -0.5)
    l2norm_heads(conv_silu(k_ref, xk_ref, wk_ref), yk_ref, 1.0)
    yv_ref[...] = conv_silu(v_ref, xv_ref, wv_ref)

    row = lax.broadcasted_iota(jnp.int32, (CHUNK, LANES), 0)
    lane = lax.broadcasted_iota(jnp.int32, (CHUNK, LANES), 1)
    left = lane < CHUNK
    lane_in = jnp.where(left, lane, lane - CHUNK)
    tri_incl = lane_in <= row
    tri_strict = lane_in < row
    eye_pair = jnp.where(lane_in == row, 1.0, 0.0)
    m_left = jnp.where(left, 1.0, 0.0).astype(BF16)
    m_right = jnp.where(left, 0.0, 1.0).astype(BF16)
    left_row = left[0:1, :]
    r2 = lax.broadcasted_iota(jnp.int32, (2 * CHUNK, CHUNK), 0)
    c2 = lax.broadcasted_iota(jnp.int32, (2 * CHUNK, CHUNK), 1)
    tri_twice = jnp.where(c2 <= jnp.where(r2 < CHUNK, r2, r2 - CHUNK), 1.0, 0.0).astype(BF16)
    dt_bias = prm_ref[0:1, :]
    neg_a = -jnp.exp(prm_ref[1:2, :])
    gn = gn_ref[...]

    def blockdiag(x_b):
        return jnp.concatenate([x_b * m_left, x_b * m_right], axis=0)

    def split_bd(x):
        hi, lo = _split2(x)
        return hi, lo, blockdiag(hi), blockdiag(lo)

    def pair_dot3(a_hi, a_lo, b_hi_bd, b_lo_bd):
        return _dot(jnp.concatenate([a_hi, a_hi, a_lo], axis=1),
                    jnp.concatenate([b_hi_bd, b_lo_bd, b_hi_bd], axis=0))

    def blockdiag2(a, b):
        z = jnp.zeros_like(a)
        return jnp.concatenate([jnp.concatenate([a, z], axis=1), jnp.concatenate([z, b], axis=1)], axis=0)

    def group_body(gi, carry):
        nc = CHUNKS_PER_STEP
        rows = [pl.ds(pl.multiple_of((gi * nc + c) * CHUNK, CHUNK), CHUNK) for c in range(nc)]
        gc, gc2_t, beta = [], [], []
        for c in range(nc):
            slab = s_ref[rows[c], :]
            z = slab + dt_bias
            g = neg_a * (jnp.maximum(z, 0.0) + jnp.log1p(jnp.exp(-jnp.abs(z))))
            beta.append(_sigmoid(slab))
            g_hi, g_mid, g_lo = _split3(g)
            gc3 = _dot(tri_twice, jnp.concatenate([g_hi, g_mid, g_lo], axis=1))
            gc2 = gc3[:, :LANES] + gc3[:, LANES:2 * LANES] + gc3[:, 2 * LANES:]
            gc.append(gc2[:CHUNK])
            gc2_t.append(gc2.T)

        slots = [(c, h) for c in range(nc) for h in range(HEADS)]
        cols = [slice(h * HEAD_DIM, (h + 1) * HEAD_DIM) for h in range(HEADS)]
        gcol = [gc[c][:, A_LANE + h:A_LANE + h + 1] for c, h in slots]
        g_last = [gc[c][CHUNK - 1:CHUNK, A_LANE + h:A_LANE + h + 1] for c, h in slots]
        e_gc = [jnp.exp(x) for x in gcol]
        e_tail = [jnp.exp(gl - x) for gl, x in zip(g_last, gcol)]
        bcol = [beta[c][:, B_LANE + h:B_LANE + h + 1] for c, h in slots]
        q = [yq_ref[rows[c], cols[h]] for c, h in slots]
        k = [yk_ref[rows[c], cols[h]] for c, h in slots]
        kb = [x * b for x, b in zip(k, bcol)]
        vb = [yv_ref[rows[c], cols[h]] * bcol[c * HEADS + h] for c, h in slots]
        k_b = [x.astype(BF16) for x in k]
        pairs = [(c, c * HEADS + 2 * p, c * HEADS + 2 * p + 1) for c in range(nc) for p in range(HEADS // 2)]

        decay, low, attn = [], [], []
        for c, a, b in pairs:
            ha, hb = a - c * HEADS, b - c * HEADS
            diff = (jnp.where(left, gcol[a], gcol[b])
                    - jnp.where(left_row, gc2_t[c][A_LANE + ha:A_LANE + ha + 1, :],
                                gc2_t[c][A_LANE + hb:A_LANE + hb + 1, :]))
            decay.append(jnp.exp(jnp.where(tri_incl, diff, NEG_INF)))
        for p, (c, a, b) in enumerate(pairs):
            lhs = jnp.concatenate([jnp.concatenate([kb[a], kb[b]], axis=1),
                                   jnp.concatenate([q[a], q[b]], axis=1)], axis=0).astype(BF16)
            kk_qk = _dot_nt(lhs, blockdiag2(k_b[a], k_b[b]))
            low.append(jnp.where(tri_strict, kk_qk[:CHUNK] * decay[p], 0.0))
            attn.append((kk_qk[CHUNK:] * decay[p]).astype(BF16))

        power = [pair_dot3(*split_bd(-l)) for l in low]
        inv = [eye_pair - l for l in low]
        for _ in range(int(math.log2(CHUNK)) - 2):
            both = []
            for s, pw in zip(inv, power):
                hi, lo = _split2(jnp.concatenate([s, pw], axis=0))
                both.append(pair_dot3(hi, lo, blockdiag(hi[CHUNK:]), blockdiag(lo[CHUNK:])))
            inv = [s + r[:CHUNK] for s, r in zip(inv, both)]
            power = [r[CHUNK:] for r in both]
        inv = [s + pair_dot3(*_split2(s), *split_bd(pw)[2:]) for s, pw in zip(inv, power)]

        u, w = [None] * len(slots), [None] * len(slots)
        for p, (c, a, b) in enumerate(pairs):
            xa_hi, xa_lo = _split2(jnp.concatenate([vb[a], kb[a] * e_gc[a]], axis=1))
            xb_hi, xb_lo = _split2(jnp.concatenate([vb[b], kb[b] * e_gc[b]], axis=1))
            x_hi = blockdiag2(xa_hi, xb_hi)
            t_hi, t_lo = _split2(inv[p])
            sol = _dot(jnp.concatenate([t_hi, t_hi, t_lo], axis=1),
                       jnp.concatenate([x_hi, blockdiag2(xa_lo, xb_lo), x_hi], axis=0))
            u[a], w[a], u[b], w[b] = (sol[:, n * HEAD_DIM:(n + 1) * HEAD_DIM] for n in range(4))
        wq = [jnp.concatenate([w[s], q[s] * e_gc[s]], axis=0).astype(BF16) for s in range(len(slots))]
        k_tail = [(k[s] * e_tail[s]).astype(BF16) for s in range(len(slots))]
        e_last = [jnp.exp(x) for x in g_last]

        for c in range(nc):
            base = c * HEADS
            state = [state_ref[h] for h in range(HEADS)]
            ws_qs = [_dot(wq[base + h], state[h].astype(BF16)) for h in range(HEADS)]
            v_new = [(u[base + h] - ws_qs[h][:CHUNK]).astype(BF16) for h in range(HEADS)]
            for h in range(HEADS):
                state_ref[h] = state[h] * e_last[base + h] + _dot_tn(k_tail[base + h], v_new[h])
            for p in range(HEADS // 2):
                a, b = 2 * p, 2 * p + 1
                o_pair = _dot(attn[c * (HEADS // 2) + p], blockdiag2(v_new[a], v_new[b]))
                for h, o_mm in ((a, o_pair[:, :HEAD_DIM]), (b, o_pair[:, HEAD_DIM:])):
                    o = ws_qs[h][CHUNK:] + o_mm
                    gate = gate_ref[rows[c], cols[h]].astype(F32)
                    on = o * lax.rsqrt(jnp.mean(o * o, axis=-1, keepdims=True) + EPS) * gn * _silu(gate)
                    o_ref[rows[c], cols[h]] = on.astype(BF16)
        return carry

    lax.fori_loop(0, blk // (CHUNK * CHUNKS_PER_STEP), group_body, 0)


def _gated_deltanet(main, small, conv_w, prm, gnorm, blk=512):
    t = main.shape[0]
    qcol = 3 * GROUP_W // GROUP_W
    big = lambda cb: pl.BlockSpec((blk, GROUP_W), lambda i: (i, cb))
    wspec = lambda cb: pl.BlockSpec((GDN_CONV, GROUP_W), lambda i: (0, cb))
    small_vec = lambda rows: pl.BlockSpec((rows, LANES), lambda i: (0, 0))
    return pl.pallas_call(
        functools.partial(_gdn_kernel, blk=blk),
        grid=(t // blk,),
        in_specs=[big(qcol), big(qcol + 1), big(qcol + 2), big(qcol + 3),
                  pl.BlockSpec((blk, LANES), lambda i: (i, 0)),
                  wspec(0), wspec(1), wspec(2),
                  small_vec(SUBLANES), small_vec(1)],
        out_specs=pl.BlockSpec((blk, GROUP_W), lambda i: (i, 0)),
        out_shape=jax.ShapeDtypeStruct((t, GROUP_W), BF16),
        scratch_shapes=[pltpu.VMEM((blk + SUBLANES, GROUP_W), F32)] * 3
                       + [pltpu.VMEM((blk, GROUP_W), F32)] * 3
                       + [pltpu.VMEM((HEADS, HEAD_DIM, HEAD_DIM), F32)],
        compiler_params=_cparams(("arbitrary",),
                                 5 * _nbytes((blk, GROUP_W), BF16) + _nbytes((blk, LANES), F32),
                                 6 * _nbytes((blk + SUBLANES, GROUP_W), F32)),
        name="gdn",
    )(main, main, main, main, small, conv_w, conv_w, conv_w, prm, gnorm)


def _outproj_kernel(oa_ref, ob_ref, w_ref, x_ref, g_ref, ga_ref, o_ref):
    y = _dot(oa_ref[...], w_ref[0:GROUP_W, :]) + _dot(ob_ref[...], w_ref[GROUP_W:2 * GROUP_W, :])
    yn = y * lax.rsqrt(jnp.mean(y * y, axis=-1, keepdims=True) + EPS) * g_ref[...]
    o_ref[...] = x_ref[...] + ga_ref[...] * yn


def _out_projection(o_a, o_b, w_out, x2, gain, gate, tm=512):
    t, d = x2.shape
    row = lambda i: (i, 0)
    vec = lambda i: (0, 0)
    return pl.pallas_call(
        _outproj_kernel,
        grid=(t // tm,),
        in_specs=[pl.BlockSpec((tm, GROUP_W), row), pl.BlockSpec((tm, GROUP_W), row),
                  pl.BlockSpec((2 * GROUP_W, d), vec),
                  pl.BlockSpec((tm, d), row),
                  pl.BlockSpec((1, d), vec), pl.BlockSpec((1, d), vec)],
        out_specs=pl.BlockSpec((tm, d), row),
        out_shape=jax.ShapeDtypeStruct((t, d), F32),
        compiler_params=_cparams(("arbitrary",),
                                 2 * _nbytes((tm, GROUP_W), BF16) + _nbytes((2 * GROUP_W, d), BF16)
                                 + 2 * _nbytes((tm, d), F32)),
        name="outproj",
    )(o_a, o_b, w_out, x2, gain, gate)


def _ffn_kernel(x_ref, g_ref, sc_ref, sh_ref, wg_ref, wv_ref, cwg_ref, cwv_ref, cbg_ref, cbv_ref, wd_ref,
                pg_ref, ga_ref, o_ref, h_ref, acc_ref, ug_ref, uv_ref, cg_ref, cv_ref, *, tm, nf):
    i = pl.program_id(0)
    j = pl.program_id(1)
    halo = SUBLANES

    @pl.when(j == 0)
    def _():
        h_ref[...] = _norm_modulate(x_ref[...], g_ref[...], sc_ref[...], sh_ref[...]).astype(BF16)
        acc_ref[...] = jnp.zeros_like(acc_ref)

    @pl.when(i == 0)
    def _():
        cg_ref[j] = jnp.zeros(cg_ref.shape[1:], F32)
        cv_ref[j] = jnp.zeros(cv_ref.shape[1:], F32)

    def up_conv(w_ref, u_ref, carry_ref, cw_ref, cb_ref):
        u = _dot(h_ref[...], w_ref[...])
        u_ref[0:halo, :] = carry_ref[j]
        u_ref[halo:halo + tm, :] = u
        carry_ref[j] = u[tm - halo:tm, :]
        y = cw_ref[FFN_CONV - 1:FFN_CONV, :] * u + cb_ref[...]
        for tap in range(FFN_CONV - 1):
            shift = FFN_CONV - 1 - tap
            y = y + cw_ref[tap:tap + 1, :] * u_ref[halo - shift:halo - shift + tm, :]
        return y

    gate = up_conv(wg_ref, ug_ref, cg_ref, cwg_ref, cbg_ref)
    val = up_conv(wv_ref, uv_ref, cv_ref, cwv_ref, cbv_ref)
    act = (gate * (1.0 / (1.0 + jnp.exp(-gate))) * val).astype(BF16)
    acc_ref[...] += _dot(act, wd_ref[...])

    @pl.when(j == nf - 1)
    def _():
        y = acc_ref[...]
        yn = y * lax.rsqrt(jnp.mean(y * y, axis=-1, keepdims=True) + EPS) * pg_ref[...]
        o_ref[...] = x_ref[...] + ga_ref[...] * yn


def _conv_ffn(x2, gain, scale, shift, w_up, conv_w, conv_b, w_down, post_gain, gate, tm=512, tf=512):
    t, d = x2.shape
    d_ff = w_down.shape[0]
    nf = d_ff // tf
    row = lambda i, j: (i, 0)
    vec = lambda i, j: (0, 0)
    gcol = lambda i, j: (0, j)
    vcol = lambda i, j: (0, nf + j)
    return pl.pallas_call(
        functools.partial(_ffn_kernel, tm=tm, nf=nf),
        grid=(t // tm, nf),
        in_specs=[pl.BlockSpec((tm, d), row),
                  pl.BlockSpec((1, d), vec), pl.BlockSpec((1, d), vec), pl.BlockSpec((1, d), vec),
                  pl.BlockSpec((d, tf), gcol), pl.BlockSpec((d, tf), vcol),
                  pl.BlockSpec((FFN_CONV, tf), gcol), pl.BlockSpec((FFN_CONV, tf), vcol),
                  pl.BlockSpec((1, tf), gcol), pl.BlockSpec((1, tf), vcol),
                  pl.BlockSpec((tf, d), lambda i, j: (j, 0)),
                  pl.BlockSpec((1, d), vec), pl.BlockSpec((1, d), vec)],
        out_specs=pl.BlockSpec((tm, d), row),
        out_shape=jax.ShapeDtypeStruct((t, d), F32),
        scratch_shapes=[pltpu.VMEM((tm, d), BF16), pltpu.VMEM((tm, d), F32),
                        pltpu.VMEM((tm + SUBLANES, tf), F32), pltpu.VMEM((tm + SUBLANES, tf), F32),
                        pltpu.VMEM((nf, SUBLANES, tf), F32), pltpu.VMEM((nf, SUBLANES, tf), F32)],
        compiler_params=_cparams(("arbitrary", "arbitrary"),
                                 2 * _nbytes((tm, d), F32) + 3 * _nbytes((d, tf), BF16),
                                 _nbytes((tm, d), BF16) + _nbytes((tm, d), F32)
                                 + 2 * _nbytes((tm + SUBLANES, tf), F32) + 2 * _nbytes((nf, SUBLANES, tf), F32)),
        name="ffn",
    )(x2, gain, scale, shift, w_up, w_up, conv_w, conv_w, conv_b, conv_b, w_down, post_gain, gate)


def _lane_row(vals, lane0):
    return jnp.zeros((1, LANES), F32).at[0, lane0:lane0 + HEADS].set(vals.astype(F32))


def _layer(x2, cond, p, attn_block):
    d = x2.shape[1]
    mod = _ada_mod(cond, p["w_ada"], p["b_ada"])
    sh_m, sc_m, ga_m, sh_f, sc_f, ga_f = (mod[:, k * d:(k + 1) * d] for k in range(6))

    w_in = p["w_in"]
    o_f = 3 * GROUP_W
    o_g = o_f + HEADS
    o_a = o_g + 3 * GROUP_W
    o_gate = o_a + 2 * HEADS
    w_t = w_in.T
    w_main = _pack_rows(w_t, ((0, o_f), (o_g, o_a - o_g), (o_gate, GROUP_W)))
    w_small = jnp.concatenate([w_t[o_f:o_g], w_t[o_a:o_gate], jnp.zeros((LANES - 3 * HEADS, d), F32)], axis=0)

    main, small = _in_projection(x2, p["norm_mix_pre"].reshape(1, d), sc_m, sh_m, w_main, w_small)

    kx, cs = _fox_prologue(small, _lane_row(p["fox_forget_bias"], F_LANE), attn_block)
    nblk = x2.shape[0] // attn_block
    cs_flat = cs.reshape(nblk, SUBLANES, LANES)[:, 0, F_LANE:F_LANE + HEADS].reshape(-1)
    o_fox = _fox_attention(cs_flat, main, kx, p["fox_out_norm"].reshape(HEAD_DIM, 1), attn_block)

    prm = jnp.concatenate([_lane_row(p["gdn_dt_bias"], A_LANE), _lane_row(p["gdn_A_log"], A_LANE),
                           jnp.zeros((SUBLANES - 2, LANES), F32)], axis=0)
    o_gdn = _gated_deltanet(main, small, p["gdn_conv_w"], prm, p["gdn_out_norm"].reshape(1, HEAD_DIM))

    x2 = _out_projection(o_fox, o_gdn, p["w_out"].astype(BF16), x2, p["norm_mix_post"].reshape(1, d), ga_m)

    return _conv_ffn(x2, p["norm_ffn_pre"].reshape(1, d), sc_f, sh_f, p["w_up"].astype(BF16),
                     p["ffn_conv_w"], p["ffn_conv_b"].reshape(1, -1), p["w_down"].astype(BF16),
                     p["norm_ffn_post"].reshape(1, d), ga_f)


def kernel(x, c, w_ada, b_ada, norm_mix_pre, norm_mix_post, w_in, fox_forget_bias, fox_out_norm, gdn_conv_w, gdn_A_log, gdn_dt_bias, gdn_out_norm, w_out, norm_ffn_pre, norm_ffn_post, w_up, ffn_conv_w, ffn_conv_b, w_down):
    b, t, d = x.shape
    assert b == 1, "single-sequence kernel"
    params = dict(w_ada=w_ada, b_ada=b_ada, norm_mix_pre=norm_mix_pre, norm_mix_post=norm_mix_post, w_in=w_in,
                  fox_forget_bias=fox_forget_bias, fox_out_norm=fox_out_norm, gdn_conv_w=gdn_conv_w,
                  gdn_A_log=gdn_A_log, gdn_dt_bias=gdn_dt_bias, gdn_out_norm=gdn_out_norm, w_out=w_out,
                  norm_ffn_pre=norm_ffn_pre, norm_ffn_post=norm_ffn_post, w_up=w_up, ffn_conv_w=ffn_conv_w,
                  ffn_conv_b=ffn_conv_b, w_down=w_down)
    attn_block = min(512, t)
    x2 = x.reshape(t, d)
    for l in range(w_ada.shape[0]):
        x2 = _layer(x2, c, {k: v[l] for k, v in params.items()}, attn_block)
    return x2.reshape(b, t, d)
```

```python
import functools
import math

import jax
import jax.numpy as jnp
from jax import lax
from jax.experimental import pallas as pl
from jax.experimental.pallas import tpu as pltpu

F32 = jnp.float32
BF16 = jnp.bfloat16

EPS = 1e-6
HEADS = 8
HEAD_DIM = 128
CHUNK = 64
CHUNKS_PER_STEP = 4
GDN_CONV = 4
FFN_CONV = 3
LANES = 128
SUBLANES = 8
GROUP_W = HEADS * HEAD_DIM
LOG2E = math.log2(math.e)
NEG_INF = float("-inf")

F_LANE, A_LANE, B_LANE = 0, HEADS, 2 * HEADS


MIB = 1024 * 1024
V7X_VMEM_BYTES = 64 * MIB
TEMP_ALLOWANCE_BYTES = 22 * MIB


def _nbytes(shape, dtype):
    return math.prod(shape) * jnp.dtype(dtype).itemsize


def _cparams(semantics, window_bytes, scratch_bytes=0, flags=None):
    want = 2 * window_bytes + scratch_bytes + TEMP_ALLOWANCE_BYTES
    return pltpu.CompilerParams(dimension_semantics=semantics, flags=flags,
                                vmem_limit_bytes=min(want, V7X_VMEM_BYTES - 4 * MIB))


def _sigmoid(x):
    return 0.5 * jnp.tanh(0.5 * x) + 0.5


def _silu(x):
    h = 0.5 * x
    return h + h * jnp.tanh(h)


def _split2(a):
    hi = a.astype(BF16)
    lo = (a - hi.astype(F32)).astype(BF16)
    return hi, lo


def _split3(a):
    hi = a.astype(BF16)
    r = a - hi.astype(F32)
    mid = r.astype(BF16)
    lo = (r - mid.astype(F32)).astype(BF16)
    return hi, mid, lo


def _dot(a, b):
    return jnp.dot(a, b, preferred_element_type=F32)


def _dot_nt(a, b):
    return lax.dot_general(a, b, (((1,), (1,)), ((), ())), preferred_element_type=F32)


def _dot_tn(a, b):
    return lax.dot_general(a, b, (((0,), (0,)), ((), ())), preferred_element_type=F32)


def _ada_kernel(c_ref, w_ref, b_ref, o_ref):
    cond = _silu(c_ref[...])
    o_ref[...] = jnp.sum(cond * w_ref[...], axis=0, keepdims=True) + b_ref[...]


def _ada_mod(c, w_ada, b_ada, tn=1024):
    d, n = w_ada.shape
    return pl.pallas_call(
        _ada_kernel,
        grid=(n // tn,),
        in_specs=[pl.BlockSpec((d, 1), lambda j: (0, 0)),
                  pl.BlockSpec((d, tn), lambda j: (0, j)),
                  pl.BlockSpec((1, tn), lambda j: (0, j))],
        out_specs=pl.BlockSpec((1, tn), lambda j: (0, j)),
        out_shape=jax.ShapeDtypeStruct((1, n), F32),
        compiler_params=_cparams(("arbitrary",), _nbytes((d, tn), F32)),
        name="ada",
    )(c.reshape(d, 1), w_ada, b_ada.reshape(1, n))


def _norm_modulate(x, gain, scale, shift):
    y = x * lax.rsqrt(jnp.mean(x * x, axis=-1, keepdims=True) + EPS) * gain
    return y * (1.0 + scale) + shift


def _pack_kernel(w_ref, o_ref):
    o_ref[...] = w_ref[...].astype(BF16)


def _pack_rows(w_t, segments, tr=1024):
    d = w_t.shape[1]
    shifts, out_rows = [], 0
    for start, size in segments:
        shifts += [start - out_rows] * (size // tr)
        out_rows += size

    assert tr % SUBLANES == 0 and all(s % SUBLANES == 0 for s in shifts)

    def src_row(j):
        off = jnp.int32(shifts[0] // SUBLANES)
        for b in range(1, len(shifts)):
            if shifts[b] != shifts[b - 1]:
                off = jnp.where(j >= b, jnp.int32(shifts[b] // SUBLANES), off)
        return (j * (tr // SUBLANES) + off) * SUBLANES

    return pl.pallas_call(
        _pack_kernel,
        grid=(out_rows // tr,),
        in_specs=[pl.BlockSpec((pl.Element(tr), pl.Element(d)), lambda j: (src_row(j), 0))],
        out_specs=pl.BlockSpec((tr, d), lambda j: (j, 0)),
        out_shape=jax.ShapeDtypeStruct((out_rows, d), BF16),
        compiler_params=_cparams(("arbitrary",), _nbytes((tr, d), F32) + _nbytes((tr, d), BF16)),
        name="pack",
    )(w_t)


def _inproj_kernel(x_ref, g_ref, sc_ref, sh_ref, wm_ref, ws_ref, om_ref, os_ref, h_ref, *, q_scale):
    j = pl.program_id(1)

    @pl.when(j == 0)
    def _():
        h = _norm_modulate(x_ref[...], g_ref[...], sc_ref[...], sh_ref[...])
        h_b = h.astype(BF16)
        h_ref[...] = h_b
        w_hi, w_lo = _split2(ws_ref[...])
        both = _dot_nt(h_b, jnp.concatenate([w_hi, w_lo], axis=0))
        os_ref[...] = both[:, :LANES] + both[:, LANES:]

    acc = _dot_nt(h_ref[...], wm_ref[...])
    om_ref[...] = (acc * jnp.where(j == 0, q_scale, 1.0)).astype(BF16)


def _in_projection(x2, gain, scale, shift, w_main, w_small, tm=1024, tn=1024):
    t, d = x2.shape
    n = w_main.shape[0]
    row = lambda i, j: (i, 0)
    vec = lambda i, j: (0, 0)
    q_scale = HEAD_DIM ** -0.5 * LOG2E
    return pl.pallas_call(
        functools.partial(_inproj_kernel, q_scale=q_scale),
        grid=(t // tm, n // tn),
        in_specs=[pl.BlockSpec((tm, d), row),
                  pl.BlockSpec((1, d), vec), pl.BlockSpec((1, d), vec), pl.BlockSpec((1, d), vec),
                  pl.BlockSpec((tn, d), lambda i, j: (j, 0)),
                  pl.BlockSpec((LANES, d), vec)],
        out_specs=[pl.BlockSpec((tm, tn), lambda i, j: (i, j)),
                   pl.BlockSpec((tm, LANES), row)],
        out_shape=[jax.ShapeDtypeStruct((t, n), BF16), jax.ShapeDtypeStruct((t, LANES), F32)],
        scratch_shapes=[pltpu.VMEM((tm, d), BF16)],
        compiler_params=_cparams(("arbitrary", "arbitrary"),
                                 _nbytes((tm, d), F32) + _nbytes((tn, d), BF16) + _nbytes((LANES, d), F32)
                                 + _nbytes((tm, tn), BF16) + _nbytes((tm, LANES), F32),
                                 _nbytes((tm, d), BF16)),
        name="inproj",
    )(x2, gain, scale, shift, w_main, w_small)


def _foxpre_kernel(s_ref, fb_ref, kx_ref, cs_ref, carry_ref, *, tb):
    i = pl.program_id(0)

    @pl.when(i == 0)
    def _():
        carry_ref[...] = jnp.zeros_like(carry_ref)

    z = s_ref[...] + fb_ref[...]
    log_f = jnp.minimum(z, 0.0) - jnp.log1p(jnp.exp(-jnp.abs(z)))
    r = lax.broadcasted_iota(jnp.int32, (tb, tb), 0)
    c = lax.broadcasted_iota(jnp.int32, (tb, tb), 1)
    tri = jnp.where(c <= r, 1.0, 0.0).astype(BF16)
    hi, mid, lo = _split3(log_f)
    cum3 = _dot(tri, jnp.concatenate([hi, mid, lo], axis=1))
    cum = cum3[:, :LANES] + cum3[:, LANES:2 * LANES] + cum3[:, 2 * LANES:]
    carry = carry_ref[...]
    cs_ref[...] = jnp.broadcast_to(carry * LOG2E, cs_ref.shape)
    carry_ref[...] = carry + cum[tb - 1:tb, :]

    dd = -cum * LOG2E
    lane = lax.broadcasted_iota(jnp.int32, (tb, HEAD_DIM), 1)
    for h in range(HEADS):
        col = jnp.broadcast_to(dd[:, F_LANE + h:F_LANE + h + 1], (tb, HEAD_DIM))
        c_hi = col.astype(BF16).astype(F32)
        c_mid = (col - c_hi).astype(BF16).astype(F32)
        c_lo = col - c_hi - c_mid
        kx = jnp.where(lane == 0, c_hi, jnp.where(lane == 1, c_mid, jnp.where(lane == 2, c_lo, 0.0)))
        kx_ref[:, h * HEAD_DIM:(h + 1) * HEAD_DIM] = kx.astype(BF16)


def _fox_prologue(small, fbias_row, tb):
    t = small.shape[0]
    nblk = t // tb
    return pl.pallas_call(
        functools.partial(_foxpre_kernel, tb=tb),
        grid=(nblk,),
        in_specs=[pl.BlockSpec((tb, LANES), lambda i: (i, 0)),
                  pl.BlockSpec((1, LANES), lambda i: (0, 0))],
        out_specs=[pl.BlockSpec((tb, GROUP_W), lambda i: (i, 0)),
                   pl.BlockSpec((SUBLANES, LANES), lambda i: (i, 0))],
        out_shape=[jax.ShapeDtypeStruct((t, GROUP_W), BF16),
                   jax.ShapeDtypeStruct((nblk * SUBLANES, LANES), F32)],
        scratch_shapes=[pltpu.VMEM((1, LANES), F32)],
        compiler_params=_cparams(("arbitrary",), _nbytes((tb, LANES), F32) + _nbytes((tb, GROUP_W), BF16)),
        name="foxpre",
    )(small, fbias_row)


BF16_SUBLANES = 16
V_ROWS = HEAD_DIM + BF16_SUBLANES
FOX_KEY_BLOCK = 512


def _fox_kernel(cs_ref, q_ref, k_ref, v_ref, kx_ref, gn_ref, o_ref,
                vt_ref, qt_ref, sa_ref, sb_ref, m_ref, acc_ref, *, tk, nblk):
    tq = 2 * tk
    h = pl.program_id(0)
    i = pl.program_id(1)

    @pl.when(i == 0)
    def _():
        row = lax.broadcasted_iota(jnp.int32, (V_ROWS - HEAD_DIM, tk), 0)
        tail = jnp.where(row == 0, 1.0, 0.0).astype(BF16)

        def fill(j, carry):
            off = pl.multiple_of(j * tk, tk)
            vt = v_ref[pl.ds(off, tk), :].astype(F32).T.astype(BF16)
            vt_ref[j] = jnp.concatenate([vt, tail], axis=0)
            return carry

        lax.fori_loop(0, nblk, fill, 0)

    row = lax.broadcasted_iota(jnp.int32, (HEAD_DIM, tq), 0)
    qt_ref[0:HEAD_DIM, :] = q_ref[...].astype(F32).T.astype(BF16)
    qt_ref[HEAD_DIM:2 * HEAD_DIM, :] = jnp.where(row < 3, 1.0, 0.0).astype(BF16)
    m_ref[...] = jnp.full_like(m_ref, NEG_INF)
    acc_ref[...] = jnp.zeros_like(acc_ref)
    first = 2 * i
    cq = cs_ref[first * HEADS + h]

    def scores(j, s_ref, q0=0):
        off = pl.multiple_of(j * tk, tk)
        k_aug = jnp.concatenate([k_ref[pl.ds(off, tk), :], kx_ref[pl.ds(off, tk), :]], axis=1)
        s_ref[:, q0:] = _dot(k_aug, qt_ref[:, q0:])

    def softmax_pv(j, s_ref, masked=False, q0=0):
        s = s_ref[:, q0:]
        if masked:
            key = lax.broadcasted_iota(jnp.int32, s.shape, 0)
            qry = lax.broadcasted_iota(jnp.int32, s.shape, 1)
            s = jnp.where(key <= qry, s, NEG_INF)
        delta = cq - cs_ref[j * HEADS + h]
        m_old = m_ref[:, q0:]
        m_new = jnp.maximum(m_old, jnp.max(s, axis=0, keepdims=True) + delta)
        p = jnp.exp2(s - (m_new - delta))
        acc_ref[:, q0:] = acc_ref[:, q0:] * jnp.exp2(m_old - m_new) + _dot(vt_ref[j], p.astype(BF16))
        m_ref[:, q0:] = m_new

    scores(0, sa_ref)

    def pair_step(j0):
        scores(j0 + 1, sb_ref)
        softmax_pv(j0, sa_ref)
        scores(j0 + 2, sa_ref)
        softmax_pv(j0 + 1, sb_ref)

    def quad_body(jj, carry):
        pair_step(4 * jj)
        pair_step(4 * jj + 2)
        return carry

    lax.fori_loop(0, lax.shift_right_logical(i, 1), quad_body, 0)

    @pl.when(lax.bitwise_and(i, 1) == 1)
    def _():
        pair_step(first - 2)

    scores(first + 1, sb_ref, q0=tk)
    softmax_pv(first, sa_ref, masked=True)
    softmax_pv(first + 1, sb_ref, masked=True, q0=tk)

    acc = acc_ref[...]
    o = acc[:HEAD_DIM, :] / acc[HEAD_DIM:HEAD_DIM + 1, :]
    o = o * lax.rsqrt(jnp.mean(o * o, axis=0, keepdims=True) + EPS) * gn_ref[...]
    o_ref[...] = o.T.astype(BF16)


def _fox_attention(cs_flat, main, kx, gnorm_col, tk):
    t = main.shape[0]
    tq = 2 * tk
    nblk = t // tk
    grid_spec = pltpu.PrefetchScalarGridSpec(
        num_scalar_prefetch=1,
        grid=(HEADS, t // tq),
        in_specs=[pl.BlockSpec((tq, HEAD_DIM), lambda h, i, cs: (i, h)),
                  pl.BlockSpec((t, HEAD_DIM), lambda h, i, cs: (0, HEADS + h)),
                  pl.BlockSpec((t, HEAD_DIM), lambda h, i, cs: (0, 2 * HEADS + h)),
                  pl.BlockSpec((t, HEAD_DIM), lambda h, i, cs: (0, h)),
                  pl.BlockSpec((HEAD_DIM, 1), lambda h, i, cs: (0, 0))],
        out_specs=pl.BlockSpec((tq, HEAD_DIM), lambda h, i, cs: (i, h)),
        scratch_shapes=[pltpu.VMEM((nblk, V_ROWS, tk), BF16), pltpu.VMEM((2 * HEAD_DIM, tq), BF16),
                        pltpu.VMEM((tk, tq), F32), pltpu.VMEM((tk, tq), F32),
                        pltpu.VMEM((1, tq), F32), pltpu.VMEM((V_ROWS, tq), F32)],
    )
    return pl.pallas_call(
        functools.partial(_fox_kernel, tk=tk, nblk=nblk),
        grid_spec=grid_spec,
        out_shape=jax.ShapeDtypeStruct((t, GROUP_W), BF16),
        compiler_params=_cparams(("arbitrary", "arbitrary"),
                                 3 * _nbytes((t, HEAD_DIM), BF16) + 2 * _nbytes((tq, HEAD_DIM), BF16),
                                 _nbytes((nblk, V_ROWS, tk), BF16) + _nbytes((2 * HEAD_DIM, tq), BF16)
                                 + 2 * _nbytes((tk, tq), F32) + _nbytes((V_ROWS + SUBLANES, tq), F32)),
        name="fox",
    )(cs_flat, main, main, main, kx, gnorm_col)


def _gdn_kernel(q_ref, k_ref, v_ref, gate_ref, s_ref, wq_ref, wk_ref, wv_ref, prm_ref, gn_ref, o_ref,
                xq_ref, xk_ref, xv_ref, yq_ref, yk_ref, yv_ref, state_ref, *, blk):
    i = pl.program_id(0)
    halo = SUBLANES

    @pl.when(i == 0)
    def _():
        state_ref[...] = jnp.zeros_like(state_ref)
        for x_ref in (xq_ref, xk_ref, xv_ref):
            x_ref[0:halo, :] = jnp.zeros((halo, GROUP_W), F32)

    def conv_silu(src_ref, x_ref, w_ref):
        x_ref[halo:halo + blk, :] = src_ref[...].astype(F32)
        y = w_ref[GDN_CONV - 1:GDN_CONV, :] * x_ref[halo:halo + blk, :]
        for tap in range(GDN_CONV - 1):
            shift = GDN_CONV - 1 - tap
            y = y + w_ref[tap:tap + 1, :] * x_ref[halo - shift:halo - shift + blk, :]
        x_ref[0:halo, :] = x_ref[blk:blk + halo, :]
        return _silu(y)

    def l2norm_heads(y, dst_ref, mult):
        for h in range(HEADS):
            yh = y[:, h * HEAD_DIM:(h + 1) * HEAD_DIM]
            inv = lax.rsqrt(jnp.sum(yh * yh, axis=-1, keepdims=True) + EPS) * mult
            dst_ref[:, h * HEAD_DIM:(h + 1) * HEAD_DIM] = yh * inv

    l2norm_heads(conv_silu(q_ref, xq_ref, wq_ref), yq_ref, HEAD_DIM ** -0.5)
    l2norm_heads(conv_silu(k_ref, xk_ref, wk_ref), yk_ref, 1.0)
    yv_ref[...] = conv_silu(v_ref, xv_ref, wv_ref)

    row = lax.broadcasted_iota(jnp.int32, (CHUNK, LANES), 0)
    lane = lax.broadcasted_iota(jnp.int32, (CHUNK, LANES), 1)
    left = lane < CHUNK
    lane_in = jnp.where(left, lane, lane - CHUNK)
    tri_incl = lane_in <= row
    tri_strict = lane_in < row
    eye_pair = jnp.where(lane_in == row, 1.0, 0.0)
    m_left = jnp.where(left, 1.0, 0.0).astype(BF16)
    m_right = jnp.where(left, 0.0, 1.0).astype(BF16)
    left_row = left[0:1, :]
    r2 = lax.broadcasted_iota(jnp.int32, (2 * CHUNK, CHUNK), 0)
    c2 = lax.broadcasted_iota(jnp.int32, (2 * CHUNK, CHUNK), 1)
    tri_twice = jnp.where(c2 <= jnp.where(r2 < CHUNK, r2, r2 - CHUNK), 1.0, 0.0).astype(BF16)
    dt_bias = prm_ref[0:1, :]
    neg_a = -jnp.exp(prm_ref[1:2, :])
    gn = gn_ref[...]

    def blockdiag(x_b):
        return jnp.concatenate([x_b * m_left, x_b * m_right], axis=0)

    def split_bd(x):
        hi, lo = _split2(x)
        return hi, lo, blockdiag(hi), blockdiag(lo)

    def pair_dot3(a_hi, a_lo, b_hi_bd, b_lo_bd):
        return _dot(jnp.concatenate([a_hi, a_hi, a_lo], axis=1),
                    jnp.concatenate([b_hi_bd, b_lo_bd, b_hi_bd], axis=0))

    def blockdiag2(a, b):
        z = jnp.zeros_like(a)
        return jnp.concatenate([jnp.concatenate([a, z], axis=1), jnp.concatenate([z, b], axis=1)], axis=0)

    def group_body(gi, carry):
        nc = CHUNKS_PER_STEP
        rows = [pl.ds(pl.multiple_of((gi * nc + c) * CHUNK, CHUNK), CHUNK) for c in range(nc)]
        gc, gc2_t, beta = [], [], []
        for c in range(nc):
            slab = s_ref[rows[c], :]
            z = slab + dt_bias
            g = neg_a * (jnp.maximum(z, 0.0) + jnp.log1p(jnp.exp(-jnp.abs(z))))
            beta.append(_sigmoid(slab))
            g_hi, g_mid, g_lo = _split3(g)
            gc3 = _dot(tri_twice, jnp.concatenate([g_hi, g_mid, g_lo], axis=1))
            gc2 = gc3[:, :LANES] + gc3[:, LANES:2 * LANES] + gc3[:, 2 * LANES:]
            gc.append(gc2[:CHUNK])
            gc2_t.append(gc2.T)

        slots = [(c, h) for c in range(nc) for h in range(HEADS)]
        cols = [slice(h * HEAD_DIM, (h + 1) * HEAD_DIM) for h in range(HEADS)]
        gcol = [gc[c][:, A_LANE + h:A_LANE + h + 1] for c, h in slots]
        g_last = [gc[c][CHUNK - 1:CHUNK, A_LANE + h:A_LANE + h + 1] for c, h in slots]
        e_gc = [jnp.exp(x) for x in gcol]
        e_tail = [jnp.exp(gl - x) for gl, x in zip(g_last, gcol)]
        bcol = [beta[c][:, B_LANE + h:B_LANE + h + 1] for c, h in slots]
        q = [yq_ref[rows[c], cols[h]] for c, h in slots]
        k = [yk_ref[rows[c], cols[h]] for c, h in slots]
        kb = [x * b for x, b in zip(k, bcol)]
        vb = [yv_ref[rows[c], cols[h]] * bcol[c * HEADS + h] for c, h in slots]
        k_b = [x.astype(BF16) for x in k]
        pairs = [(c, c * HEADS + 2 * p, c * HEADS + 2 * p + 1) for c in range(nc) for p in range(HEADS // 2)]

        decay, low, attn = [], [], []
        for c, a, b in pairs:
            ha, hb = a - c * HEADS, b - c * HEADS
            diff = (jnp.where(left, gcol[a], gcol[b])
                    - jnp.where(left_row, gc2_t[c][A_LANE + ha:A_LANE + ha + 1, :],
                                gc2_t[c][A_LANE + hb:A_LANE + hb + 1, :]))
            decay.append(jnp.exp(jnp.where(tri_incl, diff, NEG_INF)))
        for p, (c, a, b) in enumerate(pairs):
            lhs = jnp.concatenate([jnp.concatenate([kb[a], kb[b]], axis=1),
                                   jnp.concatenate([q[a], q[b]], axis=1)], axis=0).astype(BF16)
            kk_qk = _dot_nt(lhs, blockdiag2(k_b[a], k_b[b]))
            low.append(jnp.where(tri_strict, kk_qk[:CHUNK] * decay[p], 0.0))
            attn.append((kk_qk[CHUNK:] * decay[p]).astype(BF16))

        power = [pair_dot3(*split_bd(-l)) for l in low]
        inv = [eye_pair - l for l in low]
        for _ in range(int(math.log2(CHUNK)) - 2):
            both = []
            for s, pw in zip(inv, power):
                hi, lo = _split2(jnp.concatenate([s, pw], axis=0))
                both.append(pair_dot3(hi, lo, blockdiag(hi[CHUNK:]), blockdiag(lo[CHUNK:])))
            inv = [s + r[:CHUNK] for s, r in zip(inv, both)]
            power = [r[CHUNK:] for r in both]
        inv = [s + pair_dot3(*_split2(s), *split_bd(pw)[2:]) for s, pw in zip(inv, power)]

        u, w = [None] * len(slots), [None] * len(slots)
        for p, (c, a, b) in enumerate(pairs):
            xa_hi, xa_lo = _split2(jnp.concatenate([vb[a], kb[a] * e_gc[a]], axis=1))
            xb_hi, xb_lo = _split2(jnp.concatenate([vb[b], kb[b] * e_gc[b]], axis=1))
            x_hi = blockdiag2(xa_hi, xb_hi)
            t_hi, t_lo = _split2(inv[p])
            sol = _dot(jnp.concatenate([t_hi, t_hi, t_lo], axis=1),
                       jnp.concatenate([x_hi, blockdiag2(xa_lo, xb_lo), x_hi], axis=0))
            u[a], w[a], u[b], w[b] = (sol[:, n * HEAD_DIM:(n + 1) * HEAD_DIM] for n in range(4))
        wq = [jnp.concatenate([w[s], q[s] * e_gc[s]], axis=0).astype(BF16) for s in range(len(slots))]
        k_tail = [(k[s] * e_tail[s]).astype(BF16) for s in range(len(slots))]
        e_last = [jnp.exp(x) for x in g_last]

        for c in range(nc):
            base = c * HEADS
            state = [state_ref[h] for h in range(HEADS)]
            ws_qs = [_dot(wq[base + h], state[h].astype(BF16)) for h in range(HEADS)]
            v_new = [(u[base + h] - ws_qs[h][:CHUNK]).astype(BF16) for h in range(HEADS)]
            for h in range(HEADS):
                state_ref[h] = state[h] * e_last[base + h] + _dot_tn(k_tail[base + h], v_new[h])
            for p in range(HEADS // 2):
                a, b = 2 * p, 2 * p + 1
                o_pair = _dot(attn[c * (HEADS // 2) + p], blockdiag2(v_new[a], v_new[b]))
                for h, o_mm in ((a, o_pair[:, :HEAD_DIM]), (b, o_pair[:, HEAD_DIM:])):
                    o = ws_qs[h][CHUNK:] + o_mm
                    gate = gate_ref[rows[c], cols[h]].astype(F32)
                    on = o * lax.rsqrt(jnp.mean(o * o, axis=-1, keepdims=True) + EPS) * gn * _silu(gate)
                    o_ref[rows[c], cols[h]] = on.astype(BF16)
        return carry

    lax.fori_loop(0, blk // (CHUNK * CHUNKS_PER_STEP), group_body, 0)


def _gated_deltanet(main, small, conv_w, prm, gnorm, blk=512):
    t = main.shape[0]
    qcol = 3
    big = lambda cb: pl.BlockSpec((blk, GROUP_W), lambda i: (i, cb))
    wspec = lambda cb: pl.BlockSpec((GDN_CONV, GROUP_W), lambda i: (0, cb))
    small_vec = lambda rows: pl.BlockSpec((rows, LANES), lambda i: (0, 0))
    return pl.pallas_call(
        functools.partial(_gdn_kernel, blk=blk),
        grid=(t // blk,),
        in_specs=[big(qcol), big(qcol + 1), big(qcol + 2), big(qcol + 3),
                  pl.BlockSpec((blk, LANES), lambda i: (i, 0)),
                  wspec(0), wspec(1), wspec(2),
                  small_vec(SUBLANES), small_vec(1)],
        out_specs=pl.BlockSpec((blk, GROUP_W), lambda i: (i, 0)),
        out_shape=jax.ShapeDtypeStruct((t, GROUP_W), BF16),
        scratch_shapes=[pltpu.VMEM((blk + SUBLANES, GROUP_W), F32)] * 3
                       + [pltpu.VMEM((blk, GROUP_W), F32)] * 3
                       + [pltpu.VMEM((HEADS, HEAD_DIM, HEAD_DIM), F32)],
        compiler_params=_cparams(("arbitrary",),
                                 5 * _nbytes((blk, GROUP_W), BF16) + _nbytes((blk, LANES), F32),
                                 6 * _nbytes((blk + SUBLANES, GROUP_W), F32)),
        name="gdn",
    )(main, main, main, main, small, conv_w, conv_w, conv_w, prm, gnorm)


def _outproj_kernel(oa_ref, ob_ref, w_ref, x_ref, g_ref, ga_ref, o_ref):
    y = _dot(oa_ref[...], w_ref[0:GROUP_W, :]) + _dot(ob_ref[...], w_ref[GROUP_W:2 * GROUP_W, :])
    yn = y * lax.rsqrt(jnp.mean(y * y, axis=-1, keepdims=True) + EPS) * g_ref[...]
    o_ref[...] = x_ref[...] + ga_ref[...] * yn


def _out_projection(o_a, o_b, w_out, x2, gain, gate, tm=512):
    t, d = x2.shape
    row = lambda i: (i, 0)
    vec = lambda i: (0, 0)
    return pl.pallas_call(
        _outproj_kernel,
        grid=(t // tm,),
        in_specs=[pl.BlockSpec((tm, GROUP_W), row), pl.BlockSpec((tm, GROUP_W), row),
                  pl.BlockSpec((2 * GROUP_W, d), vec),
                  pl.BlockSpec((tm, d), row),
                  pl.BlockSpec((1, d), vec), pl.BlockSpec((1, d), vec)],
        out_specs=pl.BlockSpec((tm, d), row),
        out_shape=jax.ShapeDtypeStruct((t, d), F32),
        compiler_params=_cparams(("arbitrary",),
                                 2 * _nbytes((tm, GROUP_W), BF16) + _nbytes((2 * GROUP_W, d), BF16)
                                 + 2 * _nbytes((tm, d), F32)),
        name="outproj",
    )(o_a, o_b, w_out, x2, gain, gate)


def _ffn_kernel(x_ref, g_ref, sc_ref, sh_ref, wg_ref, wv_ref, cwg_ref, cwv_ref, cbg_ref, cbv_ref, wd_ref,
                pg_ref, ga_ref, o_ref, h_ref, acc_ref, ug_ref, uv_ref, cg_ref, cv_ref, *, tm, nf):
    i = pl.program_id(0)
    j = pl.program_id(1)
    halo = SUBLANES

    @pl.when(j == 0)
    def _():
        h_ref[...] = _norm_modulate(x_ref[...], g_ref[...], sc_ref[...], sh_ref[...]).astype(BF16)
        acc_ref[...] = jnp.zeros_like(acc_ref)

    @pl.when(i == 0)
    def _():
        cg_ref[j] = jnp.zeros(cg_ref.shape[1:], F32)
        cv_ref[j] = jnp.zeros(cv_ref.shape[1:], F32)

    def up_conv(w_ref, u_ref, carry_ref, cw_ref, cb_ref):
        u = _dot(h_ref[...], w_ref[...])
        u_ref[0:halo, :] = carry_ref[j]
        u_ref[halo:halo + tm, :] = u
        carry_ref[j] = u[tm - halo:tm, :]
        y = cw_ref[FFN_CONV - 1:FFN_CONV, :] * u + cb_ref[...]
        for tap in range(FFN_CONV - 1):
            shift = FFN_CONV - 1 - tap
            y = y + cw_ref[tap:tap + 1, :] * u_ref[halo - shift:halo - shift + tm, :]
        return y

    gate = up_conv(wg_ref, ug_ref, cg_ref, cwg_ref, cbg_ref)
    val = up_conv(wv_ref, uv_ref, cv_ref, cwv_ref, cbv_ref)
    act = (gate * (1.0 / (1.0 + jnp.exp(-gate))) * val).astype(BF16)
    acc_ref[...] += _dot(act, wd_ref[...])

    @pl.when(j == nf - 1)
    def _():
        y = acc_ref[...]
        yn = y * lax.rsqrt(jnp.mean(y * y, axis=-1, keepdims=True) + EPS) * pg_ref[...]
        o_ref[...] = x_ref[...] + ga_ref[...] * yn


def _conv_ffn(x2, gain, scale, shift, w_up, conv_w, conv_b, w_down, post_gain, gate, tm=512, tf=512):
    t, d = x2.shape
    d_ff = w_down.shape[0]
    nf = d_ff // tf
    row = lambda i, j: (i, 0)
    vec = lambda i, j: (0, 0)
    gcol = lambda i, j: (0, j)
    vcol = lambda i, j: (0, nf + j)
    return pl.pallas_call(
        functools.partial(_ffn_kernel, tm=tm, nf=nf),
        grid=(t // tm, nf),
        in_specs=[pl.BlockSpec((tm, d), row),
                  pl.BlockSpec((1, d), vec), pl.BlockSpec((1, d), vec), pl.BlockSpec((1, d), vec),
                  pl.BlockSpec((d, tf), gcol), pl.BlockSpec((d, tf), vcol),
                  pl.BlockSpec((FFN_CONV, tf), gcol), pl.BlockSpec((FFN_CONV, tf), vcol),
                  pl.BlockSpec((1, tf), gcol), pl.BlockSpec((1, tf), vcol),
                  pl.BlockSpec((tf, d), lambda i, j: (j, 0)),
                  pl.BlockSpec((1, d), vec), pl.BlockSpec((1, d), vec)],
        out_specs=pl.BlockSpec((tm, d), row),
        out_shape=jax.ShapeDtypeStruct((t, d), F32),
        scratch_shapes=[pltpu.VMEM((tm, d), BF16), pltpu.VMEM((tm, d), F32),
                        pltpu.VMEM((tm + SUBLANES, tf), F32), pltpu.VMEM((tm + SUBLANES, tf), F32),
                        pltpu.VMEM((nf, SUBLANES, tf), F32), pltpu.VMEM((nf, SUBLANES, tf), F32)],
        compiler_params=_cparams(("arbitrary", "arbitrary"),
                                 2 * _nbytes((tm, d), F32) + 3 * _nbytes((d, tf), BF16),
                                 _nbytes((tm, d), BF16) + _nbytes((tm, d), F32)
                                 + 2 * _nbytes((tm + SUBLANES, tf), F32) + 2 * _nbytes((nf, SUBLANES, tf), F32)),
        name="ffn",
    )(x2, gain, scale, shift, w_up, w_up, conv_w, conv_w, conv_b, conv_b, w_down, post_gain, gate)


def _lane_row(vals, lane0):
    return jnp.zeros((1, LANES), F32).at[0, lane0:lane0 + HEADS].set(vals.astype(F32))


def _layer(x2, cond, p, attn_block):
    d = x2.shape[1]
    mod = _ada_mod(cond, p["w_ada"], p["b_ada"])
    sh_m, sc_m, ga_m, sh_f, sc_f, ga_f = (mod[:, k * d:(k + 1) * d] for k in range(6))

    w_in = p["w_in"]
    o_f = 3 * GROUP_W
    o_g = o_f + HEADS
    o_a = o_g + 3 * GROUP_W
    o_gate = o_a + 2 * HEADS
    w_t = w_in.T
    w_main = _pack_rows(w_t, ((0, o_f), (o_g, o_a - o_g), (o_gate, GROUP_W)))
    w_small = jnp.concatenate([w_t[o_f:o_g], w_t[o_a:o_gate], jnp.zeros((LANES - 3 * HEADS, d), F32)], axis=0)

    main, small = _in_projection(x2, p["norm_mix_pre"].reshape(1, d), sc_m, sh_m, w_main, w_small)

    kx, cs = _fox_prologue(small, _lane_row(p["fox_forget_bias"], F_LANE), attn_block)
    nblk = x2.shape[0] // attn_block
    cs_flat = cs.reshape(nblk, SUBLANES, LANES)[:, 0, F_LANE:F_LANE + HEADS].reshape(-1)
    o_fox = _fox_attention(cs_flat, main, kx, p["fox_out_norm"].reshape(HEAD_DIM, 1), attn_block)

    prm = jnp.concatenate([_lane_row(p["gdn_dt_bias"], A_LANE), _lane_row(p["gdn_A_log"], A_LANE),
                           jnp.zeros((SUBLANES - 2, LANES), F32)], axis=0)
    o_gdn = _gated_deltanet(main, small, p["gdn_conv_w"], prm, p["gdn_out_norm"].reshape(1, HEAD_DIM))

    x2 = _out_projection(o_fox, o_gdn, p["w_out"].astype(BF16), x2, p["norm_mix_post"].reshape(1, d), ga_m)

    return _conv_ffn(x2, p["norm_ffn_pre"].reshape(1, d), sc_f, sh_f, p["w_up"].astype(BF16),
                     p["ffn_conv_w"], p["ffn_conv_b"].reshape(1, -1), p["w_down"].astype(BF16),
                     p["norm_ffn_post"].reshape(1, d), ga_f)


def kernel(x, c, w_ada, b_ada, norm_mix_pre, norm_mix_post, w_in, fox_forget_bias, fox_out_norm, gdn_conv_w, gdn_A_log, gdn_dt_bias, gdn_out_norm, w_out, norm_ffn_pre, norm_ffn_post, w_up, ffn_conv_w, ffn_conv_b, w_down):
    b, t, d = x.shape
    assert b == 1, "single-sequence kernel"
    assert t % (2 * FOX_KEY_BLOCK) == 0, "sequence length must be a multiple of the largest row tile"
    params = dict(w_ada=w_ada, b_ada=b_ada, norm_mix_pre=norm_mix_pre, norm_mix_post=norm_mix_post, w_in=w_in,
                  fox_forget_bias=fox_forget_bias, fox_out_norm=fox_out_norm, gdn_conv_w=gdn_conv_w,
                  gdn_A_log=gdn_A_log, gdn_dt_bias=gdn_dt_bias, gdn_out_norm=gdn_out_norm, w_out=w_out,
                  norm_ffn_pre=norm_ffn_pre, norm_ffn_post=norm_ffn_post, w_up=w_up, ffn_conv_w=ffn_conv_w,
                  ffn_conv_b=ffn_conv_b, w_down=w_down)
    attn_block = min(FOX_KEY_BLOCK, t // 2)
    x2 = x.reshape(t, d)
    for l in range(w_ada.shape[0]):
        x2 = _layer(x2, c, {k: v[l] for k, v in params.items()}, attn_block)
    return x2.reshape(b, t, d)
```

```python
import functools
import math

import jax
import jax.numpy as jnp
from jax import lax
from jax.experimental import pallas as pl
from jax.experimental.pallas import tpu as pltpu

F32 = jnp.float32
BF16 = jnp.bfloat16

EPS = 1e-6
HEADS = 8
HEAD_DIM = 128
CHUNK = 64
CHUNKS_PER_STEP = 4
GDN_CONV = 4
FFN_CONV = 3
LANES = 128
SUBLANES = 8
GROUP_W = HEADS * HEAD_DIM
LOG2E = math.log2(math.e)
NEG_INF = float("-inf")

F_LANE, A_LANE, B_LANE = 0, HEADS, 2 * HEADS


MIB = 1024 * 1024
V7X_VMEM_BYTES = 64 * MIB
TEMP_ALLOWANCE_BYTES = 22 * MIB


def _nbytes(shape, dtype):
    return math.prod(shape) * jnp.dtype(dtype).itemsize


def _cparams(semantics, window_bytes, scratch_bytes=0, flags=None):
    want = 2 * window_bytes + scratch_bytes + TEMP_ALLOWANCE_BYTES
    return pltpu.CompilerParams(dimension_semantics=semantics, flags=flags,
                                vmem_limit_bytes=min(want, V7X_VMEM_BYTES - 4 * MIB))


def _sigmoid(x):
    return 0.5 * jnp.tanh(0.5 * x) + 0.5


def _silu(x):
    h = 0.5 * x
    return h + h * jnp.tanh(h)


def _split2(a):
    hi = a.astype(BF16)
    lo = (a - hi.astype(F32)).astype(BF16)
    return hi, lo


def _split3(a):
    hi = a.astype(BF16)
    r = a - hi.astype(F32)
    mid = r.astype(BF16)
    lo = (r - mid.astype(F32)).astype(BF16)
    return hi, mid, lo


def _dot(a, b):
    return jnp.dot(a, b, preferred_element_type=F32)


def _dot_nt(a, b):
    return lax.dot_general(a, b, (((1,), (1,)), ((), ())), preferred_element_type=F32)


def _dot_tn(a, b):
    return lax.dot_general(a, b, (((0,), (0,)), ((), ())), preferred_element_type=F32)


def _ada_kernel(c_ref, w_ref, b_ref, o_ref):
    cond = _silu(c_ref[...])
    o_ref[...] = jnp.sum(cond * w_ref[...], axis=0, keepdims=True) + b_ref[...]


def _ada_mod(c, w_ada, b_ada, tn=1024):
    d, n = w_ada.shape
    return pl.pallas_call(
        _ada_kernel,
        grid=(n // tn,),
        in_specs=[pl.BlockSpec((d, 1), lambda j: (0, 0)),
                  pl.BlockSpec((d, tn), lambda j: (0, j)),
                  pl.BlockSpec((1, tn), lambda j: (0, j))],
        out_specs=pl.BlockSpec((1, tn), lambda j: (0, j)),
        out_shape=jax.ShapeDtypeStruct((1, n), F32),
        compiler_params=_cparams(("arbitrary",), _nbytes((d, tn), F32)),
        name="ada",
    )(c.reshape(d, 1), w_ada, b_ada.reshape(1, n))


def _norm_modulate(x, gain, scale, shift):
    y = x * lax.rsqrt(jnp.mean(x * x, axis=-1, keepdims=True) + EPS) * gain
    return y * (1.0 + scale) + shift


def _pack_kernel(w_ref, o_ref):
    o_ref[...] = w_ref[...].astype(BF16)


def _pack_rows(w_t, segments, tr=1024):
    d = w_t.shape[1]
    shifts, out_rows = [], 0
    for start, size in segments:
        shifts += [start - out_rows] * (size // tr)
        out_rows += size

    assert tr % SUBLANES == 0 and all(s % SUBLANES == 0 for s in shifts)

    def src_row(j):
        off = jnp.int32(shifts[0] // SUBLANES)
        for b in range(1, len(shifts)):
            if shifts[b] != shifts[b - 1]:
                off = jnp.where(j >= b, jnp.int32(shifts[b] // SUBLANES), off)
        return (j * (tr // SUBLANES) + off) * SUBLANES

    return pl.pallas_call(
        _pack_kernel,
        grid=(out_rows // tr,),
        in_specs=[pl.BlockSpec((pl.Element(tr), pl.Element(d)), lambda j: (src_row(j), 0))],
        out_specs=pl.BlockSpec((tr, d), lambda j: (j, 0)),
        out_shape=jax.ShapeDtypeStruct((out_rows, d), BF16),
        compiler_params=_cparams(("arbitrary",), _nbytes((tr, d), F32) + _nbytes((tr, d), BF16)),
        name="pack",
    )(w_t)


def _inproj_kernel(x_ref, g_ref, sc_ref, sh_ref, wm_ref, ws_ref, om_ref, os_ref, h_ref, *, q_scale):
    j = pl.program_id(1)

    @pl.when(j == 0)
    def _():
        h = _norm_modulate(x_ref[...], g_ref[...], sc_ref[...], sh_ref[...])
        h_b = h.astype(BF16)
        h_ref[...] = h_b
        w_hi, w_lo = _split2(ws_ref[...])
        both = _dot_nt(h_b, jnp.concatenate([w_hi, w_lo], axis=0))
        os_ref[...] = both[:, :LANES] + both[:, LANES:]

    acc = _dot_nt(h_ref[...], wm_ref[...])
    om_ref[...] = (acc * jnp.where(j == 0, q_scale, 1.0)).astype(BF16)


def _in_projection(x2, gain, scale, shift, w_main, w_small, tm=1024, tn=1024):
    t, d = x2.shape
    n = w_main.shape[0]
    row = lambda i, j: (i, 0)
    vec = lambda i, j: (0, 0)
    q_scale = HEAD_DIM ** -0.5 * LOG2E
    return pl.pallas_call(
        functools.partial(_inproj_kernel, q_scale=q_scale),
        grid=(t // tm, n // tn),
        in_specs=[pl.BlockSpec((tm, d), row),
                  pl.BlockSpec((1, d), vec), pl.BlockSpec((1, d), vec), pl.BlockSpec((1, d), vec),
                  pl.BlockSpec((tn, d), lambda i, j: (j, 0)),
                  pl.BlockSpec((LANES, d), vec)],
        out_specs=[pl.BlockSpec((tm, tn), lambda i, j: (i, j)),
                   pl.BlockSpec((tm, LANES), row)],
        out_shape=[jax.ShapeDtypeStruct((t, n), BF16), jax.ShapeDtypeStruct((t, LANES), F32)],
        scratch_shapes=[pltpu.VMEM((tm, d), BF16)],
        compiler_params=_cparams(("arbitrary", "arbitrary"),
                                 _nbytes((tm, d), F32) + _nbytes((tn, d), BF16) + _nbytes((LANES, d), F32)
                                 + _nbytes((tm, tn), BF16) + _nbytes((tm, LANES), F32),
                                 _nbytes((tm, d), BF16)),
        name="inproj",
    )(x2, gain, scale, shift, w_main, w_small)


def _foxpre_kernel(s_ref, fb_ref, kx_ref, cs_ref, carry_ref, *, tb):
    i = pl.program_id(0)

    @pl.when(i == 0)
    def _():
        carry_ref[...] = jnp.zeros_like(carry_ref)

    z = s_ref[...] + fb_ref[...]
    log_f = jnp.minimum(z, 0.0) - jnp.log1p(jnp.exp(-jnp.abs(z)))
    r = lax.broadcasted_iota(jnp.int32, (tb, tb), 0)
    c = lax.broadcasted_iota(jnp.int32, (tb, tb), 1)
    tri = jnp.where(c <= r, 1.0, 0.0).astype(BF16)
    hi, mid, lo = _split3(log_f)
    cum3 = _dot(tri, jnp.concatenate([hi, mid, lo], axis=1))
    cum = cum3[:, :LANES] + cum3[:, LANES:2 * LANES] + cum3[:, 2 * LANES:]
    carry = carry_ref[...]
    cs_ref[...] = jnp.broadcast_to(carry * LOG2E, cs_ref.shape)
    carry_ref[...] = carry + cum[tb - 1:tb, :]

    dd = -cum * LOG2E
    lane = lax.broadcasted_iota(jnp.int32, (tb, HEAD_DIM), 1)
    for h in range(HEADS):
        col = jnp.broadcast_to(dd[:, F_LANE + h:F_LANE + h + 1], (tb, HEAD_DIM))
        c_hi = col.astype(BF16).astype(F32)
        c_mid = (col - c_hi).astype(BF16).astype(F32)
        c_lo = col - c_hi - c_mid
        kx = jnp.where(lane == 0, c_hi, jnp.where(lane == 1, c_mid, jnp.where(lane == 2, c_lo, 0.0)))
        kx_ref[:, h * HEAD_DIM:(h + 1) * HEAD_DIM] = kx.astype(BF16)


def _fox_prologue(small, fbias_row, tb):
    t = small.shape[0]
    nblk = t // tb
    return pl.pallas_call(
        functools.partial(_foxpre_kernel, tb=tb),
        grid=(nblk,),
        in_specs=[pl.BlockSpec((tb, LANES), lambda i: (i, 0)),
                  pl.BlockSpec((1, LANES), lambda i: (0, 0))],
        out_specs=[pl.BlockSpec((tb, GROUP_W), lambda i: (i, 0)),
                   pl.BlockSpec((SUBLANES, LANES), lambda i: (i, 0))],
        out_shape=[jax.ShapeDtypeStruct((t, GROUP_W), BF16),
                   jax.ShapeDtypeStruct((nblk * SUBLANES, LANES), F32)],
        scratch_shapes=[pltpu.VMEM((1, LANES), F32)],
        compiler_params=_cparams(("arbitrary",), _nbytes((tb, LANES), F32) + _nbytes((tb, GROUP_W), BF16)),
        name="foxpre",
    )(small, fbias_row)


V_ROWS = HEAD_DIM + 16


def _fox_kernel(cs_ref, q_ref, k_ref, v_ref, kx_ref, gn_ref, o_ref,
                vt_ref, qt_ref, sa_ref, sb_ref, m_ref, acc_ref, *, tk, nblk):
    tq = 2 * tk
    h = pl.program_id(0)
    i = pl.program_id(1)

    @pl.when(i == 0)
    def _():
        row = lax.broadcasted_iota(jnp.int32, (V_ROWS - HEAD_DIM, tk), 0)
        tail = jnp.where(row == 0, 1.0, 0.0).astype(BF16)

        def fill(j, carry):
            off = pl.multiple_of(j * tk, tk)
            vt = v_ref[pl.ds(off, tk), :].astype(F32).T.astype(BF16)
            vt_ref[j] = jnp.concatenate([vt, tail], axis=0)
            return carry

        lax.fori_loop(0, nblk, fill, 0)

    row = lax.broadcasted_iota(jnp.int32, (HEAD_DIM, tq), 0)
    qt_ref[0:HEAD_DIM, :] = q_ref[...].astype(F32).T.astype(BF16)
    qt_ref[HEAD_DIM:2 * HEAD_DIM, :] = jnp.where(row < 3, 1.0, 0.0).astype(BF16)
    m_ref[...] = jnp.full_like(m_ref, NEG_INF)
    acc_ref[...] = jnp.zeros_like(acc_ref)
    first = 2 * i
    cq = cs_ref[first * HEADS + h]

    def scores(j, s_ref, q0=0):
        off = pl.multiple_of(j * tk, tk)
        k_aug = jnp.concatenate([k_ref[pl.ds(off, tk), :], kx_ref[pl.ds(off, tk), :]], axis=1)
        s_ref[:, q0:] = _dot(k_aug, qt_ref[:, q0:])

    def softmax_pv(j, s_ref, masked=False, q0=0):
        s = s_ref[:, q0:]
        if masked:
            key = lax.broadcasted_iota(jnp.int32, s.shape, 0)
            qry = lax.broadcasted_iota(jnp.int32, s.shape, 1)
            s = jnp.where(key <= qry, s, NEG_INF)
        delta = cq - cs_ref[j * HEADS + h]
        m_old = m_ref[:, q0:]
        m_new = jnp.maximum(m_old, jnp.max(s, axis=0, keepdims=True) + delta)
        p = jnp.exp2(s - (m_new - delta))
        acc_ref[:, q0:] = acc_ref[:, q0:] * jnp.exp2(m_old - m_new) + _dot(vt_ref[j], p.astype(BF16))
        m_ref[:, q0:] = m_new

    scores(0, sa_ref)

    def pair_step(j0):
        scores(j0 + 1, sb_ref)
        softmax_pv(j0, sa_ref)
        scores(j0 + 2, sa_ref)
        softmax_pv(j0 + 1, sb_ref)

    def quad_body(jj, carry):
        pair_step(4 * jj)
        pair_step(4 * jj + 2)
        return carry

    lax.fori_loop(0, lax.shift_right_logical(i, 1), quad_body, 0)

    @pl.when(lax.bitwise_and(i, 1) == 1)
    def _():
        pair_step(first - 2)

    scores(first + 1, sb_ref, q0=tk)
    softmax_pv(first, sa_ref, masked=True)
    softmax_pv(first + 1, sb_ref, masked=True, q0=tk)

    acc = acc_ref[...]
    o = acc[:HEAD_DIM, :] / acc[HEAD_DIM:HEAD_DIM + 1, :]
    o = o * lax.rsqrt(jnp.mean(o * o, axis=0, keepdims=True) + EPS) * gn_ref[...]
    o_ref[...] = o.T.astype(BF16)


def _fox_attention(cs_flat, main, kx, gnorm_col, tk):
    t = main.shape[0]
    tq = 2 * tk
    nblk = t // tk
    grid_spec = pltpu.PrefetchScalarGridSpec(
        num_scalar_prefetch=1,
        grid=(HEADS, t // tq),
        in_specs=[pl.BlockSpec((tq, HEAD_DIM), lambda h, i, cs: (i, h)),
                  pl.BlockSpec((t, HEAD_DIM), lambda h, i, cs: (0, HEADS + h)),
                  pl.BlockSpec((t, HEAD_DIM), lambda h, i, cs: (0, 2 * HEADS + h)),
                  pl.BlockSpec((t, HEAD_DIM), lambda h, i, cs: (0, h)),
                  pl.BlockSpec((HEAD_DIM, 1), lambda h, i, cs: (0, 0))],
        out_specs=pl.BlockSpec((tq, HEAD_DIM), lambda h, i, cs: (i, h)),
        scratch_shapes=[pltpu.VMEM((nblk, V_ROWS, tk), BF16), pltpu.VMEM((2 * HEAD_DIM, tq), BF16),
                        pltpu.VMEM((tk, tq), F32), pltpu.VMEM((tk, tq), F32),
                        pltpu.VMEM((1, tq), F32), pltpu.VMEM((V_ROWS, tq), F32)],
    )
    return pl.pallas_call(
        functools.partial(_fox_kernel, tk=tk, nblk=nblk),
        grid_spec=grid_spec,
        out_shape=jax.ShapeDtypeStruct((t, GROUP_W), BF16),
        compiler_params=_cparams(("arbitrary", "arbitrary"),
                                 3 * _nbytes((t, HEAD_DIM), BF16) + 2 * _nbytes((tq, HEAD_DIM), BF16),
                                 _nbytes((nblk, V_ROWS, tk), BF16) + _nbytes((2 * HEAD_DIM, tq), BF16)
                                 + 2 * _nbytes((tk, tq), F32) + _nbytes((V_ROWS + SUBLANES, tq), F32)),
        name="fox",
    )(cs_flat, main, main, main, kx, gnorm_col)


def _gdn_kernel(q_ref, k_ref, v_ref, gate_ref, s_ref, wq_ref, wk_ref, wv_ref, prm_ref, gn_ref, o_ref,
                xq_ref, xk_ref, xv_ref, yq_ref, yk_ref, yv_ref, state_ref, *, blk):
    i = pl.program_id(0)
    halo = SUBLANES

    @pl.when(i == 0)
    def _():
        state_ref[...] = jnp.zeros_like(state_ref)
        for x_ref in (xq_ref, xk_ref, xv_ref):
            x_ref[0:halo, :] = jnp.zeros((halo, GROUP_W), F32)

    for src_ref, x_ref in ((q_ref, xq_ref), (k_ref, xk_ref), (v_ref, xv_ref)):
        x_ref[halo:halo + blk, :] = src_ref[...].astype(F32)

    def conv_rows(ci, carry):
        r0 = pl.multiple_of(ci * CHUNK, CHUNK)
        for x_ref, w_ref, dst_ref, mult in ((xq_ref, wq_ref, yq_ref, HEAD_DIM ** -0.5),
                                            (xk_ref, wk_ref, yk_ref, 1.0), (xv_ref, wv_ref, yv_ref, None)):
            for h in range(HEADS):
                cols = slice(h * HEAD_DIM, (h + 1) * HEAD_DIM)
                xw = x_ref[pl.ds(r0, CHUNK + halo), cols]
                y = w_ref[GDN_CONV - 1:GDN_CONV, cols] * xw[halo:]
                for tap in range(GDN_CONV - 1):
                    shift = GDN_CONV - 1 - tap
                    y = y + w_ref[tap:tap + 1, cols] * xw[halo - shift:halo - shift + CHUNK]
                y = _silu(y)
                if mult is not None:
                    y = y * (lax.rsqrt(jnp.sum(y * y, axis=-1, keepdims=True) + EPS) * mult)
                dst_ref[pl.ds(r0, CHUNK), cols] = y
        return carry

    lax.fori_loop(0, blk // CHUNK, conv_rows, 0)
    for x_ref in (xq_ref, xk_ref, xv_ref):
        x_ref[0:halo, :] = x_ref[blk:blk + halo, :]

    row = lax.broadcasted_iota(jnp.int32, (CHUNK, LANES), 0)
    lane = lax.broadcasted_iota(jnp.int32, (CHUNK, LANES), 1)
    left = lane < CHUNK
    lane_in = jnp.where(left, lane, lane - CHUNK)
    tri_incl = lane_in <= row
    tri_strict = lane_in < row
    eye_pair = jnp.where(lane_in == row, 1.0, 0.0)
    m_left = jnp.where(left, 1.0, 0.0).astype(BF16)
    m_right = jnp.where(left, 0.0, 1.0).astype(BF16)
    left_row = left[0:1, :]
    r2 = lax.broadcasted_iota(jnp.int32, (2 * CHUNK, CHUNK), 0)
    c2 = lax.broadcasted_iota(jnp.int32, (2 * CHUNK, CHUNK), 1)
    tri_twice = jnp.where(c2 <= jnp.where(r2 < CHUNK, r2, r2 - CHUNK), 1.0, 0.0).astype(BF16)
    dt_bias = prm_ref[0:1, :]
    neg_a = -jnp.exp(prm_ref[1:2, :])
    gn = gn_ref[...]

    def blockdiag(x_b):
        return jnp.concatenate([x_b * m_left, x_b * m_right], axis=0)

    def split_bd(x):
        hi, lo = _split2(x)
        return hi, lo, blockdiag(hi), blockdiag(lo)

    def pair_dot3(a_hi, a_lo, b_hi_bd, b_lo_bd):
        return _dot(jnp.concatenate([a_hi, a_hi, a_lo], axis=1),
                    jnp.concatenate([b_hi_bd, b_lo_bd, b_hi_bd], axis=0))

    def blockdiag2(a, b):
        z = jnp.zeros_like(a)
        return jnp.concatenate([jnp.concatenate([a, z], axis=1), jnp.concatenate([z, b], axis=1)], axis=0)

    def group_body(gi, carry):
        nc = CHUNKS_PER_STEP
        rows = [pl.ds(pl.multiple_of((gi * nc + c) * CHUNK, CHUNK), CHUNK) for c in range(nc)]
        gc, gc2_t, beta = [], [], []
        for c in range(nc):
            slab = s_ref[rows[c], :]
            z = slab + dt_bias
            g = neg_a * (jnp.maximum(z, 0.0) + jnp.log1p(jnp.exp(-jnp.abs(z))))
            beta.append(_sigmoid(slab))
            g_hi, g_mid, g_lo = _split3(g)
            gc3 = _dot(tri_twice, jnp.concatenate([g_hi, g_mid, g_lo], axis=1))
            gc2 = gc3[:, :LANES] + gc3[:, LANES:2 * LANES] + gc3[:, 2 * LANES:]
            gc.append(gc2[:CHUNK])
            gc2_t.append(gc2.T)

        slots = [(c, h) for c in range(nc) for h in range(HEADS)]
        cols = [slice(h * HEAD_DIM, (h + 1) * HEAD_DIM) for h in range(HEADS)]
        gcol = [gc[c][:, A_LANE + h:A_LANE + h + 1] for c, h in slots]
        g_last = [gc[c][CHUNK - 1:CHUNK, A_LANE + h:A_LANE + h + 1] for c, h in slots]
        e_gc = [jnp.exp(x) for x in gcol]
        e_tail = [jnp.exp(gl - x) for gl, x in zip(g_last, gcol)]
        bcol = [beta[c][:, B_LANE + h:B_LANE + h + 1] for c, h in slots]
        q = [yq_ref[rows[c], cols[h]] for c, h in slots]
        k = [yk_ref[rows[c], cols[h]] for c, h in slots]
        kb = [x * b for x, b in zip(k, bcol)]
        vb = [yv_ref[rows[c], cols[h]] * bcol[c * HEADS + h] for c, h in slots]
        k_b = [x.astype(BF16) for x in k]
        pairs = [(c, c * HEADS + 2 * p, c * HEADS + 2 * p + 1) for c in range(nc) for p in range(HEADS // 2)]

        decay, low, attn = [], [], []
        for c, a, b in pairs:
            ha, hb = a - c * HEADS, b - c * HEADS
            diff = (jnp.where(left, gcol[a], gcol[b])
                    - jnp.where(left_row, gc2_t[c][A_LANE + ha:A_LANE + ha + 1, :],
                                gc2_t[c][A_LANE + hb:A_LANE + hb + 1, :]))
            decay.append(jnp.exp(jnp.where(tri_incl, diff, NEG_INF)))
        for p, (c, a, b) in enumerate(pairs):
            lhs = jnp.concatenate([jnp.concatenate([kb[a], kb[b]], axis=1),
                                   jnp.concatenate([q[a], q[b]], axis=1)], axis=0).astype(BF16)
            kk_qk = _dot_nt(lhs, blockdiag2(k_b[a], k_b[b]))
            low.append(jnp.where(tri_strict, kk_qk[:CHUNK] * decay[p], 0.0))
            attn.append((kk_qk[CHUNK:] * decay[p]).astype(BF16))

        power = [pair_dot3(*split_bd(-l)) for l in low]
        inv = [eye_pair - l for l in low]
        for _ in range(int(math.log2(CHUNK)) - 2):
            both = []
            for s, pw in zip(inv, power):
                hi, lo = _split2(jnp.concatenate([s, pw], axis=0))
                both.append(pair_dot3(hi, lo, blockdiag(hi[CHUNK:]), blockdiag(lo[CHUNK:])))
            inv = [s + r[:CHUNK] for s, r in zip(inv, both)]
            power = [r[CHUNK:] for r in both]
        inv = [s + pair_dot3(*_split2(s), *split_bd(pw)[2:]) for s, pw in zip(inv, power)]

        u, w = [None] * len(slots), [None] * len(slots)
        for p, (c, a, b) in enumerate(pairs):
            xa_hi, xa_lo = _split2(jnp.concatenate([vb[a], kb[a] * e_gc[a]], axis=1))
            xb_hi, xb_lo = _split2(jnp.concatenate([vb[b], kb[b] * e_gc[b]], axis=1))
            x_hi = blockdiag2(xa_hi, xb_hi)
            t_hi, t_lo = _split2(inv[p])
            sol = _dot(jnp.concatenate([t_hi, t_hi, t_lo], axis=1),
                       jnp.concatenate([x_hi, blockdiag2(xa_lo, xb_lo), x_hi], axis=0))
            u[a], w[a], u[b], w[b] = (sol[:, n * HEAD_DIM:(n + 1) * HEAD_DIM] for n in range(4))
        wq = [jnp.concatenate([w[s], q[s] * e_gc[s]], axis=0).astype(BF16) for s in range(len(slots))]
        k_tail = [(k[s] * e_tail[s]).astype(BF16) for s in range(len(slots))]
        e_last = [jnp.exp(x) for x in g_last]

        for c in range(nc):
            base = c * HEADS
            state = [state_ref[h] for h in range(HEADS)]
            ws_qs = [_dot(wq[base + h], state[h].astype(BF16)) for h in range(HEADS)]
            v_new = [(u[base + h] - ws_qs[h][:CHUNK]).astype(BF16) for h in range(HEADS)]
            for h in range(HEADS):
                state_ref[h] = state[h] * e_last[base + h] + _dot_tn(k_tail[base + h], v_new[h])
            for p in range(HEADS // 2):
                a, b = 2 * p, 2 * p + 1
                o_pair = _dot(attn[c * (HEADS // 2) + p], blockdiag2(v_new[a], v_new[b]))
                for h, o_mm in ((a, o_pair[:, :HEAD_DIM]), (b, o_pair[:, HEAD_DIM:])):
                    o = ws_qs[h][CHUNK:] + o_mm
                    gate = gate_ref[rows[c], cols[h]].astype(F32)
                    on = o * lax.rsqrt(jnp.mean(o * o, axis=-1, keepdims=True) + EPS) * gn * _silu(gate)
                    o_ref[rows[c], cols[h]] = on.astype(BF16)
        return carry

    lax.fori_loop(0, blk // (CHUNK * CHUNKS_PER_STEP), group_body, 0)


def _gated_deltanet(main, small, conv_w, prm, gnorm, blk=512):
    t = main.shape[0]
    qcol = 3 * GROUP_W // GROUP_W
    big = lambda cb: pl.BlockSpec((blk, GROUP_W), lambda i: (i, cb))
    wspec = lambda cb: pl.BlockSpec((GDN_CONV, GROUP_W), lambda i: (0, cb))
    small_vec = lambda rows: pl.BlockSpec((rows, LANES), lambda i: (0, 0))
    return pl.pallas_call(
        functools.partial(_gdn_kernel, blk=blk),
        grid=(t // blk,),
        in_specs=[big(qcol), big(qcol + 1), big(qcol + 2), big(qcol + 3),
                  pl.BlockSpec((blk, LANES), lambda i: (i, 0)),
                  wspec(0), wspec(1), wspec(2),
                  small_vec(SUBLANES), small_vec(1)],
        out_specs=pl.BlockSpec((blk, GROUP_W), lambda i: (i, 0)),
        out_shape=jax.ShapeDtypeStruct((t, GROUP_W), BF16),
        scratch_shapes=[pltpu.VMEM((blk + SUBLANES, GROUP_W), F32)] * 3
                       + [pltpu.VMEM((blk, GROUP_W), F32)] * 3
                       + [pltpu.VMEM((HEADS, HEAD_DIM, HEAD_DIM), F32)],
        compiler_params=_cparams(("arbitrary",),
                                 5 * _nbytes((blk, GROUP_W), BF16) + _nbytes((blk, LANES), F32),
                                 6 * _nbytes((blk + SUBLANES, GROUP_W), F32)),
        name="gdn",
    )(main, main, main, main, small, conv_w, conv_w, conv_w, prm, gnorm)


def _outproj_kernel(oa_ref, ob_ref, w_ref, x_ref, g_ref, ga_ref, o_ref):
    y = _dot(oa_ref[...], w_ref[0:GROUP_W, :]) + _dot(ob_ref[...], w_ref[GROUP_W:2 * GROUP_W, :])
    yn = y * lax.rsqrt(jnp.mean(y * y, axis=-1, keepdims=True) + EPS) * g_ref[...]
    o_ref[...] = x_ref[...] + ga_ref[...] * yn


def _out_projection(o_a, o_b, w_out, x2, gain, gate, tm=512):
    t, d = x2.shape
    row = lambda i: (i, 0)
    vec = lambda i: (0, 0)
    return pl.pallas_call(
        _outproj_kernel,
        grid=(t // tm,),
        in_specs=[pl.BlockSpec((tm, GROUP_W), row), pl.BlockSpec((tm, GROUP_W), row),
                  pl.BlockSpec((2 * GROUP_W, d), vec),
                  pl.BlockSpec((tm, d), row),
                  pl.BlockSpec((1, d), vec), pl.BlockSpec((1, d), vec)],
        out_specs=pl.BlockSpec((tm, d), row),
        out_shape=jax.ShapeDtypeStruct((t, d), F32),
        compiler_params=_cparams(("arbitrary",),
                                 2 * _nbytes((tm, GROUP_W), BF16) + _nbytes((2 * GROUP_W, d), BF16)
                                 + 2 * _nbytes((tm, d), F32)),
        name="outproj",
    )(o_a, o_b, w_out, x2, gain, gate)


def _ffn_kernel(x_ref, g_ref, sc_ref, sh_ref, wg_ref, wv_ref, cwg_ref, cwv_ref, cbg_ref, cbv_ref, wd_ref,
                pg_ref, ga_ref, o_ref, h_ref, acc_ref, ug_ref, uv_ref, cg_ref, cv_ref, *, tm, nf):
    i = pl.program_id(0)
    j = pl.program_id(1)
    halo = SUBLANES

    @pl.when(j == 0)
    def _():
        h_ref[...] = _norm_modulate(x_ref[...], g_ref[...], sc_ref[...], sh_ref[...]).astype(BF16)
        acc_ref[...] = jnp.zeros_like(acc_ref)

    @pl.when(i == 0)
    def _():
        cg_ref[j] = jnp.zeros(cg_ref.shape[1:], F32)
        cv_ref[j] = jnp.zeros(cv_ref.shape[1:], F32)

    def up_conv(w_ref, u_ref, carry_ref, cw_ref, cb_ref):
        u = _dot(h_ref[...], w_ref[...])
        u_ref[0:halo, :] = carry_ref[j]
        u_ref[halo:halo + tm, :] = u
        carry_ref[j] = u[tm - halo:tm, :]
        y = cw_ref[FFN_CONV - 1:FFN_CONV, :] * u + cb_ref[...]
        for tap in range(FFN_CONV - 1):
            shift = FFN_CONV - 1 - tap
            y = y + cw_ref[tap:tap + 1, :] * u_ref[halo - shift:halo - shift + tm, :]
        return y

    gate = up_conv(wg_ref, ug_ref, cg_ref, cwg_ref, cbg_ref)
    val = up_conv(wv_ref, uv_ref, cv_ref, cwv_ref, cbv_ref)
    act = (gate * (1.0 / (1.0 + jnp.exp(-gate))) * val).astype(BF16)
    acc_ref[...] += _dot(act, wd_ref[...])

    @pl.when(j == nf - 1)
    def _():
        y = acc_ref[...]
        yn = y * lax.rsqrt(jnp.mean(y * y, axis=-1, keepdims=True) + EPS) * pg_ref[...]
        o_ref[...] = x_ref[...] + ga_ref[...] * yn


def _conv_ffn(x2, gain, scale, shift, w_up, conv_w, conv_b, w_down, post_gain, gate, tm=512, tf=512):
    t, d = x2.shape
    d_ff = w_down.shape[0]
    nf = d_ff // tf
    row = lambda i, j: (i, 0)
    vec = lambda i, j: (0, 0)
    gcol = lambda i, j: (0, j)
    vcol = lambda i, j: (0, nf + j)
    return pl.pallas_call(
        functools.partial(_ffn_kernel, tm=tm, nf=nf),
        grid=(t // tm, nf),
        in_specs=[pl.BlockSpec((tm, d), row),
                  pl.BlockSpec((1, d), vec), pl.BlockSpec((1, d), vec), pl.BlockSpec((1, d), vec),
                  pl.BlockSpec((d, tf), gcol), pl.BlockSpec((d, tf), vcol),
                  pl.BlockSpec((FFN_CONV, tf), gcol), pl.BlockSpec((FFN_CONV, tf), vcol),
                  pl.BlockSpec((1, tf), gcol), pl.BlockSpec((1, tf), vcol),
                  pl.BlockSpec((tf, d), lambda i, j: (j, 0)),
                  pl.BlockSpec((1, d), vec), pl.BlockSpec((1, d), vec)],
        out_specs=pl.BlockSpec((tm, d), row),
        out_shape=jax.ShapeDtypeStruct((t, d), F32),
        scratch_shapes=[pltpu.VMEM((tm, d), BF16), pltpu.VMEM((tm, d), F32),
                        pltpu.VMEM((tm + SUBLANES, tf), F32), pltpu.VMEM((tm + SUBLANES, tf), F32),
                        pltpu.VMEM((nf, SUBLANES, tf), F32), pltpu.VMEM((nf, SUBLANES, tf), F32)],
        compiler_params=_cparams(("arbitrary", "arbitrary"),
                                 2 * _nbytes((tm, d), F32) + 3 * _nbytes((d, tf), BF16),
                                 _nbytes((tm, d), BF16) + _nbytes((tm, d), F32)
                                 + 2 * _nbytes((tm + SUBLANES, tf), F32) + 2 * _nbytes((nf, SUBLANES, tf), F32)),
        name="ffn",
    )(x2, gain, scale, shift, w_up, w_up, conv_w, conv_w, conv_b, conv_b, w_down, post_gain, gate)


def _lane_row(vals, lane0):
    return jnp.zeros((1, LANES), F32).at[0, lane0:lane0 + HEADS].set(vals.astype(F32))


def _layer(x2, cond, p, attn_block):
    d = x2.shape[1]
    mod = _ada_mod(cond, p["w_ada"], p["b_ada"])
    sh_m, sc_m, ga_m, sh_f, sc_f, ga_f = (mod[:, k * d:(k + 1) * d] for k in range(6))

    w_in = p["w_in"]
    o_f = 3 * GROUP_W
    o_g = o_f + HEADS
    o_a = o_g + 3 * GROUP_W
    o_gate = o_a + 2 * HEADS
    w_t = w_in.T
    w_main = _pack_rows(w_t, ((0, o_f), (o_g, o_a - o_g), (o_gate, GROUP_W)))
    w_small = jnp.concatenate([w_t[o_f:o_g], w_t[o_a:o_gate], jnp.zeros((LANES - 3 * HEADS, d), F32)], axis=0)

    main, small = _in_projection(x2, p["norm_mix_pre"].reshape(1, d), sc_m, sh_m, w_main, w_small)

    kx, cs = _fox_prologue(small, _lane_row(p["fox_forget_bias"], F_LANE), attn_block)
    nblk = x2.shape[0] // attn_block
    cs_flat = cs.reshape(nblk, SUBLANES, LANES)[:, 0, F_LANE:F_LANE + HEADS].reshape(-1)
    o_fox = _fox_attention(cs_flat, main, kx, p["fox_out_norm"].reshape(HEAD_DIM, 1), attn_block)

    prm = jnp.concatenate([_lane_row(p["gdn_dt_bias"], A_LANE), _lane_row(p["gdn_A_log"], A_LANE),
                           jnp.zeros((SUBLANES - 2, LANES), F32)], axis=0)
    o_gdn = _gated_deltanet(main, small, p["gdn_conv_w"], prm, p["gdn_out_norm"].reshape(1, HEAD_DIM))

    x2 = _out_projection(o_fox, o_gdn, p["w_out"].astype(BF16), x2, p["norm_mix_post"].reshape(1, d), ga_m)

    return _conv_ffn(x2, p["norm_ffn_pre"].reshape(1, d), sc_f, sh_f, p["w_up"].astype(BF16),
                     p["ffn_conv_w"], p["ffn_conv_b"].reshape(1, -1), p["w_down"].astype(BF16),
                     p["norm_ffn_post"].reshape(1, d), ga_f)


def kernel(x, c, w_ada, b_ada, norm_mix_pre, norm_mix_post, w_in, fox_forget_bias, fox_out_norm, gdn_conv_w, gdn_A_log, gdn_dt_bias, gdn_out_norm, w_out, norm_ffn_pre, norm_ffn_post, w_up, ffn_conv_w, ffn_conv_b, w_down):
    b, t, d = x.shape
    assert b == 1, "single-sequence kernel"
    params = dict(w_ada=w_ada, b_ada=b_ada, norm_mix_pre=norm_mix_pre, norm_mix_post=norm_mix_post, w_in=w_in,
                  fox_forget_bias=fox_forget_bias, fox_out_norm=fox_out_norm, gdn_conv_w=gdn_conv_w,
                  gdn_A_log=gdn_A_log, gdn_dt_bias=gdn_dt_bias, gdn_out_norm=gdn_out_norm, w_out=w_out,
                  norm_ffn_pre=norm_ffn_pre, norm_ffn_post=norm_ffn_post, w_up=w_up, ffn_conv_w=ffn_conv_w,
                  ffn_conv_b=ffn_conv_b, w_down=w_down)
    attn_block = min(512, t)
    x2 = x.reshape(t, d)
    for l in range(w_ada.shape[0]):
        x2 = _layer(x2, c, {k: v[l] for k, v in params.items()}, attn_block)
    return x2.reshape(b, t, d)
```
